```python
import math
import jax, jax.numpy as jnp
from jax import lax
import numpy as np

D_MODEL = 1024
BATCH = 16
SEQ = 2048
DEPTH = 4

N_A_LAYERS = DEPTH // 2
N_B_LAYERS = DEPTH - N_A_LAYERS
HEAD_DIM = 64
MEM_LEN = 256
MEM_HEADS = 4
MEM_WIDTH = MEM_HEADS * HEAD_DIM
MIX_WIDTH = D_MODEL - MEM_WIDTH
LRU_WIDTH = MIX_WIDTH
LRU_BLOCKS = LRU_WIDTH // HEAD_DIM
LRU_BLOCK = LRU_WIDTH // LRU_BLOCKS
LRU_CONV = 4
LRU_C = 8.0
SWA_HEADS = MIX_WIDTH // HEAD_DIM
SWA_KV_HEADS = 4
SWA_GROUP = SWA_HEADS // SWA_KV_HEADS
WINDOW = 128
BLOCK = 128
D_FF = 2816
FFN_CONV = 3
EPS = 1e-6

kernel_name = "hawk_yoco_swa_sink_alibi_hybrid"


def rmsnorm(x, g):
    xf = x.astype(jnp.float32)
    y = xf * lax.rsqrt(jnp.mean(xf * xf, axis=-1, keepdims=True) + EPS)
    return (y * g.astype(jnp.float32)).astype(x.dtype)


def causal_dwconv(x, w, b):
    width, ch = w.shape
    y = lax.conv_general_dilated(
        x, w[:, None, :].astype(x.dtype), window_strides=(1,), padding=[(width - 1, 0)],
        dimension_numbers=("NWC", "WIO", "NWC"), feature_group_count=ch)
    return y + b.astype(x.dtype)


def alibi_slopes(n):
    def pow2_slopes(m):
        start = 2.0 ** (-8.0 / m)
        return [start ** (i + 1) for i in range(m)]
    c = 2 ** int(math.floor(math.log2(n)))
    s = pow2_slopes(c)
    if c != n:
        s = s + pow2_slopes(2 * c)[0::2][: n - c]
    return np.asarray(s, dtype=np.float32)


def rglru(u_x, u_gate, w_conv, b_conv, w_r, b_r, w_i, b_i, lam):
    bsz, t, _ = u_x.shape
    xc = causal_dwconv(u_x, w_conv, b_conv)
    xb = xc.reshape(bsz, t, LRU_BLOCKS, LRU_BLOCK)
    r = jax.nn.sigmoid(jnp.einsum("btni,nij->btnj", xb, w_r) + b_r).reshape(bsz, t, LRU_WIDTH)
    i = jax.nn.sigmoid(jnp.einsum("btni,nij->btnj", xb, w_i) + b_i).reshape(bsz, t, LRU_WIDTH)
    log_a = -LRU_C * r.astype(jnp.float32) * jax.nn.softplus(-lam.astype(jnp.float32))
    a = jnp.exp(log_a)
    b = jnp.sqrt(-jnp.expm1(2.0 * log_a)) * (i * xc).astype(jnp.float32)

    def combine(lhs, rhs):
        a1, b1 = lhs
        a2, b2 = rhs
        return a1 * a2, a2 * b1 + b2

    _, h = lax.associative_scan(combine, (a, b), axis=1)
    return h.astype(u_x.dtype) * jax.nn.gelu(u_gate, approximate=True)


def band_blocks(t):
    bsz, seq, nh, hd = t.shape
    tb = t.reshape(bsz, seq // BLOCK, BLOCK, nh, hd)
    prev = jnp.pad(tb[:, :-1], ((0, 0), (1, 0), (0, 0), (0, 0), (0, 0)))
    return jnp.concatenate([prev, tb], axis=2)


def swa_sink_attention(q, k_blk, v_blk, sinks, slopes):
    bsz, seq, _, hd = q.shape
    nb = seq // BLOCK
    qb = q.reshape(bsz, nb, BLOCK, SWA_KV_HEADS, SWA_GROUP, hd)
    s = jnp.einsum("bnqkgd,bnskd->bnkgqs", qb, k_blk,
                   preferred_element_type=jnp.float32) * (hd ** -0.5)
    q_pos = jnp.arange(BLOCK)[:, None] + BLOCK
    k_pos = jnp.arange(2 * BLOCK)[None, :]
    dist = q_pos - k_pos
    in_window = (dist >= 0) & (dist < WINDOW)
    has_prev = (jnp.arange(nb)[:, None, None] > 0) | (k_pos[None] >= BLOCK)
    mask = in_window[None] & has_prev
    alibi = -slopes.reshape(SWA_KV_HEADS, SWA_GROUP, 1, 1) * dist.astype(jnp.float32)
    s = jnp.where(mask[None, :, None, None], s + alibi, -jnp.inf)
    sink = sinks.astype(jnp.float32).reshape(1, 1, SWA_KV_HEADS, SWA_GROUP, 1, 1)
    m = jnp.maximum(jnp.max(s, axis=-1, keepdims=True), sink)
    p = jnp.exp(s - m)
    p = p / (jnp.sum(p, axis=-1, keepdims=True) + jnp.exp(sink - m))
    o = jnp.einsum("bnkgqs,bnskd->bnqkgd", p.astype(v_blk.dtype), v_blk)
    return o.reshape(bsz, seq, SWA_HEADS * hd)


def memory_attention(q, k, v):
    s = jnp.einsum("bthd,bmhd->bhtm", q, k, preferred_element_type=jnp.float32) * (q.shape[-1] ** -0.5)
    p = jax.nn.softmax(s, axis=-1)
    o = jnp.einsum("bhtm,bmhd->bthd", p.astype(v.dtype), v)
    return o.reshape(q.shape[0], q.shape[1], MEM_WIDTH)


def conv_gated_ffn(h, w_up, w_conv, b_conv, w_down):
    u = causal_dwconv(h @ w_up, w_conv, b_conv)
    g, v = jnp.split(u, 2, axis=-1)
    return (jax.nn.gelu(g, approximate=True) * v) @ w_down


def setup_inputs(seed: int = 0) -> dict:
    key = jax.random.key(seed)
    ks = list(jax.random.split(key, 25))

    def nrm(k, shape, scale):
        return jax.random.normal(k, shape, jnp.float32) * scale

    din = D_MODEL ** -0.5
    x = nrm(ks[0], (BATCH, SEQ, D_MODEL), 1.0)
    mem = nrm(ks[1], (BATCH, MEM_LEN, D_MODEL), 1.0)
    g_mix_pre = 1.0 + nrm(ks[2], (DEPTH, D_MODEL), 0.02)
    g_mix_post = 1.0 + nrm(ks[3], (DEPTH, D_MODEL), 0.02)
    g_ffn_pre = 1.0 + nrm(ks[4], (DEPTH, D_MODEL), 0.02)
    g_ffn_post = 1.0 + nrm(ks[5], (DEPTH, D_MODEL), 0.02)
    g_mem = 1.0 + nrm(ks[6], (DEPTH, D_MODEL), 0.02)
    w_mem_kv = nrm(ks[7], (DEPTH, D_MODEL, 2 * MEM_WIDTH), din)
    w_mix_out = nrm(ks[8], (DEPTH, MIX_WIDTH + MEM_WIDTH, D_MODEL), (MIX_WIDTH + MEM_WIDTH) ** -0.5)
    w_ffn_up = nrm(ks[9], (DEPTH, D_MODEL, 2 * D_FF), din)
    w_ffn_conv = nrm(ks[10], (DEPTH, FFN_CONV, 2 * D_FF), FFN_CONV ** -0.5)
    b_ffn_conv = nrm(ks[11], (DEPTH, 2 * D_FF), 0.01)
    w_ffn_down = nrm(ks[12], (DEPTH, D_FF, D_MODEL), D_FF ** -0.5)
    w_in_a = nrm(ks[13], (N_A_LAYERS, D_MODEL, 2 * LRU_WIDTH + MEM_WIDTH), din)
    w_conv_a = nrm(ks[14], (N_A_LAYERS, LRU_CONV, LRU_WIDTH), LRU_CONV ** -0.5)
    b_conv_a = nrm(ks[15], (N_A_LAYERS, LRU_WIDTH), 0.01)
    w_rg_r = nrm(ks[16], (N_A_LAYERS, LRU_BLOCKS, LRU_BLOCK, LRU_BLOCK), LRU_BLOCK ** -0.5)
    b_rg_r = nrm(ks[17], (N_A_LAYERS, LRU_BLOCKS, LRU_BLOCK), 0.01)
    w_rg_i = nrm(ks[18], (N_A_LAYERS, LRU_BLOCKS, LRU_BLOCK, LRU_BLOCK), LRU_BLOCK ** -0.5)
    b_rg_i = nrm(ks[19], (N_A_LAYERS, LRU_BLOCKS, LRU_BLOCK), 0.01)
    a_base = jax.random.uniform(ks[20], (N_A_LAYERS, LRU_WIDTH), jnp.float32, 0.9, 0.999) ** (1.0 / LRU_C)
    lru_lambda = jnp.log(a_base) - jnp.log1p(-a_base)
    w_in_b = nrm(ks[21], (N_B_LAYERS, D_MODEL, MIX_WIDTH + MEM_WIDTH), din)
    sinks_b = nrm(ks[22], (N_B_LAYERS, SWA_HEADS), 0.5)
    g_kv = 1.0 + nrm(ks[23], (D_MODEL,), 0.02)
    w_kv = nrm(ks[24], (D_MODEL, 2 * SWA_KV_HEADS * HEAD_DIM), din)
    return {"x": x, "mem": mem, "g_mix_pre": g_mix_pre, "g_mix_post": g_mix_post,
            "g_ffn_pre": g_ffn_pre, "g_ffn_post": g_ffn_post, "g_mem": g_mem,
            "w_mem_kv": w_mem_kv, "w_mix_out": w_mix_out, "w_ffn_up": w_ffn_up,
            "w_ffn_conv": w_ffn_conv, "b_ffn_conv": b_ffn_conv, "w_ffn_down": w_ffn_down,
            "w_in_a": w_in_a, "w_conv_a": w_conv_a, "b_conv_a": b_conv_a,
            "w_rg_r": w_rg_r, "b_rg_r": b_rg_r, "w_rg_i": w_rg_i, "b_rg_i": b_rg_i,
            "lru_lambda": lru_lambda, "w_in_b": w_in_b, "sinks_b": sinks_b,
            "g_kv": g_kv, "w_kv": w_kv}


def reference(x, mem, g_mix_pre, g_mix_post, g_ffn_pre, g_ffn_post, g_mem, w_mem_kv, w_mix_out,
              w_ffn_up, w_ffn_conv, b_ffn_conv, w_ffn_down, w_in_a, w_conv_a, b_conv_a,
              w_rg_r, b_rg_r, w_rg_i, b_rg_i, lru_lambda, w_in_b, sinks_b, g_kv, w_kv):
    bsz, seq, _ = x.shape
    mem_len = mem.shape[1]
    slopes = jnp.asarray(alibi_slopes(SWA_HEADS))
    k_blk = None
    v_blk = None
    for layer in range(DEPTH):
        if layer == N_A_LAYERS:
            kv = (rmsnorm(x, g_kv) @ w_kv).reshape(bsz, seq, 2, SWA_KV_HEADS, HEAD_DIM)
            k_blk = band_blocks(kv[:, :, 0])
            v_blk = band_blocks(kv[:, :, 1])

        h = rmsnorm(x, g_mix_pre[layer])
        mkv = (rmsnorm(mem, g_mem[layer]) @ w_mem_kv[layer]).reshape(bsz, mem_len, 2, MEM_HEADS, HEAD_DIM)
        if layer < N_A_LAYERS:
            j = layer
            proj = h @ w_in_a[j]
            u_gate, u_x, q_mem = jnp.split(proj, [LRU_WIDTH, 2 * LRU_WIDTH], axis=-1)
            y_main = rglru(u_x, u_gate, w_conv_a[j], b_conv_a[j], w_rg_r[j], b_rg_r[j],
                           w_rg_i[j], b_rg_i[j], lru_lambda[j])
        else:
            j = layer - N_A_LAYERS
            proj = h @ w_in_b[j]
            q_swa, q_mem = jnp.split(proj, [MIX_WIDTH], axis=-1)
            y_main = swa_sink_attention(q_swa.reshape(bsz, seq, SWA_HEADS, HEAD_DIM),
                                        k_blk, v_blk, sinks_b[j], slopes)
        y_mem = memory_attention(q_mem.reshape(bsz, seq, MEM_HEADS, HEAD_DIM), mkv[:, :, 0], mkv[:, :, 1])
        y = jnp.concatenate([y_main, y_mem], axis=-1) @ w_mix_out[layer]
        x = x + rmsnorm(y, g_mix_post[layer])

        h = rmsnorm(x, g_ffn_pre[layer])
        f = conv_gated_ffn(h, w_ffn_up[layer], w_ffn_conv[layer], b_ffn_conv[layer], w_ffn_down[layer])
        x = x + rmsnorm(f, g_ffn_post[layer])
    return x
```

```python
import functools
import math

import jax
import jax.numpy as jnp
import numpy as np
from jax import lax
from jax.experimental import pallas as pl
from jax.experimental.pallas import tpu as pltpu

D_MODEL = 1024
DEPTH = 4
N_A_LAYERS = DEPTH // 2
HEAD_DIM = 64
MEM_LEN = 256
MEM_HEADS = 4
MEM_WIDTH = MEM_HEADS * HEAD_DIM
MIX_WIDTH = D_MODEL - MEM_WIDTH
LRU_WIDTH = MIX_WIDTH
LRU_BLOCK = 64
LRU_CONV = 4
LRU_C = 8.0
SWA_HEADS = MIX_WIDTH // HEAD_DIM
SWA_KV_HEADS = 4
SWA_GROUP = SWA_HEADS // SWA_KV_HEADS
KV_WIDTH = SWA_KV_HEADS * HEAD_DIM
WINDOW = 128
D_FF = 2816
FFN_CONV = 3
EPS = 1e-6

SUBLANES = 8
MXU_DIM = 256
VMEM_LIMIT_BYTES = 56 * 1024 * 1024

GATE_CHUNKS = LRU_WIDTH // MXU_DIM
FFN_CHUNK = 256
FFN_CHUNKS = D_FF // FFN_CHUNK

TM_MIX = 256
TM_FFN = 256
TM_KV = 512

BF16 = jnp.bfloat16
F32 = jnp.float32


def _alibi_slopes(n):
    def pow2_slopes(m):
        start = 2.0 ** (-8.0 / m)
        return [start ** (i + 1) for i in range(m)]
    c = 2 ** int(math.floor(math.log2(n)))
    s = pow2_slopes(c)
    if c != n:
        s = s + pow2_slopes(2 * c)[0::2][: n - c]
    return [float(np.float32(v)) for v in s]


SLOPES = _alibi_slopes(SWA_HEADS)


def _rmsnorm(x, g):
    ms = jnp.mean(x * x, axis=-1, keepdims=True)
    return x * lax.rsqrt(ms + EPS) * g


def _gelu_tanh(x):
    c = math.sqrt(2.0 / math.pi)
    return 0.5 * x * (1.0 + jnp.tanh(c * (x + 0.044715 * (x * x * x))))


def _dot(a, b):
    return jnp.dot(a, b, preferred_element_type=F32)


def _dot_nt(a, b):
    return lax.dot_general(a, b, (((1,), (1,)), ((), ())), preferred_element_type=F32)


def _memory_attention(q_mem, mk_ref, mv_ref):
    outs = []
    for hh in range(MEM_HEADS):
        sl = slice(hh * HEAD_DIM, (hh + 1) * HEAD_DIM)
        q = (q_mem[:, sl] * (HEAD_DIM ** -0.5)).astype(BF16)
        s = _dot_nt(q, mk_ref[:, sl])
        m = jnp.max(s, axis=-1, keepdims=True)
        p = jnp.exp(s - m)
        p = p * (1.0 / jnp.sum(p, axis=-1, keepdims=True))
        outs.append(_dot(p.astype(BF16), mv_ref[:, sl]))
    return jnp.concatenate(outs, axis=-1)


def _mix_out(x, y_main, y_mem, wout_ref, gpost_ref):
    y = _dot(y_main.astype(BF16), wout_ref[:MIX_WIDTH, :])
    y = y + _dot(y_mem.astype(BF16), wout_ref[MIX_WIDTH:, :])
    return x + _rmsnorm(y, gpost_ref[...])


def _memkv_kernel(mem_ref, g_ref, w_ref, o_ref):
    m = mem_ref[...]
    ms = jnp.mean(m * m, axis=-1, keepdims=True)
    mn = m * lax.rsqrt(ms + EPS)
    for l in range(DEPTH):
        o_ref[l] = _dot((mn * g_ref[l]).astype(BF16), w_ref[l]).astype(BF16)


def _memkv(mem, g_mem, w_mem_kv_b):
    bsz = mem.shape[0]
    return pl.pallas_call(
        _memkv_kernel,
        grid=(bsz,),
        in_specs=[
            pl.BlockSpec((None, MEM_LEN, D_MODEL), lambda b: (b, 0, 0)),
            pl.BlockSpec((DEPTH, 1, D_MODEL), lambda b: (0, 0, 0)),
            pl.BlockSpec((DEPTH, D_MODEL, 2 * MEM_WIDTH), lambda b: (0, 0, 0)),
        ],
        out_specs=pl.BlockSpec((DEPTH, None, MEM_LEN, 2 * MEM_WIDTH), lambda b: (0, b, 0, 0)),
        out_shape=jax.ShapeDtypeStruct((DEPTH, bsz, MEM_LEN, 2 * MEM_WIDTH), BF16),
        compiler_params=pltpu.CompilerParams(dimension_semantics=("arbitrary",)),
        name="memkv",
    )(mem, g_mem, w_mem_kv_b)


def _kv_kernel(x_ref, g_ref, w_ref, o_ref):
    h = _rmsnorm(x_ref[...], g_ref[...])
    o_ref[...] = _dot(h.astype(BF16), w_ref[...]).astype(BF16)


def _shared_kv(x, g_kv, w_kv_b):
    bsz, seq, _ = x.shape
    return pl.pallas_call(
        _kv_kernel,
        grid=(bsz, seq // TM_KV),
        in_specs=[
            pl.BlockSpec((None, TM_KV, D_MODEL), lambda b, t: (b, t, 0)),
            pl.BlockSpec((1, D_MODEL), lambda b, t: (0, 0)),
            pl.BlockSpec((D_MODEL, 2 * KV_WIDTH), lambda b, t: (0, 0)),
        ],
        out_specs=pl.BlockSpec((None, TM_KV, 2 * KV_WIDTH), lambda b, t: (b, t, 0)),
        out_shape=jax.ShapeDtypeStruct((bsz, seq, 2 * KV_WIDTH), BF16),
        compiler_params=pltpu.CompilerParams(dimension_semantics=("arbitrary", "arbitrary")),
        name="shared_kv",
    )(x, g_kv, w_kv_b)


def _rglru_kernel(x_ref, gpre_ref, win_ref, wconv_ref, bconv_ref, wgate_ref, bgate_ref,
                  lam_ref, mk_ref, mv_ref, wout_ref, gpost_ref, o_ref,
                  ubuf, a_buf, b_buf, h_buf, hcarry):
    tm = x_ref.shape[0]
    pad = SUBLANES

    @pl.when(pl.program_id(1) == 0)
    def _():
        ubuf[0:pad, :] = jnp.zeros((pad, LRU_WIDTH), F32)
        hcarry[...] = jnp.zeros_like(hcarry)

    x = x_ref[...]
    h = _rmsnorm(x, gpre_ref[...]).astype(BF16)
    proj = _dot(h, win_ref[...])
    u_gate = proj[:, :LRU_WIDTH]
    u_x = proj[:, LRU_WIDTH:2 * LRU_WIDTH]
    q_mem = proj[:, 2 * LRU_WIDTH:]

    ubuf[pad:pad + tm, :] = u_x
    xc = u_x * wconv_ref[LRU_CONV - 1:LRU_CONV, :] + bconv_ref[...]
    for k in range(LRU_CONV - 1):
        shift = LRU_CONV - 1 - k
        xc = xc + ubuf[pad - shift:pad - shift + tm, :] * wconv_ref[k:k + 1, :]
    ubuf[0:pad, :] = u_x[tm - pad:, :]

    xcb = xc.astype(BF16)
    rs, is_ = [], []
    for c in range(GATE_CHUNKS):
        g = _dot(xcb[:, c * MXU_DIM:(c + 1) * MXU_DIM], wgate_ref[c]) + bgate_ref[c]
        rs.append(jax.nn.sigmoid(g[:, :MXU_DIM]))
        is_.append(jax.nn.sigmoid(g[:, MXU_DIM:]))
    r = jnp.concatenate(rs, axis=-1)
    i = jnp.concatenate(is_, axis=-1)

    lam = lam_ref[...]
    softplus_neg_lam = jnp.maximum(-lam, 0.0) + jnp.log1p(jnp.exp(-jnp.abs(lam)))
    log_a = (-LRU_C) * r * softplus_neg_lam
    a = jnp.exp(log_a)
    gain = jnp.sqrt(-jnp.tanh(log_a) * (a * a + 1.0))
    a_buf[...] = a
    b_buf[...] = gain * (i * xc)

    row = lax.broadcasted_iota(jnp.int32, (SUBLANES, LRU_WIDTH), 0)

    def group(gi, carry):
        r0 = pl.multiple_of(gi * SUBLANES, SUBLANES)
        ag = a_buf[pl.ds(r0, SUBLANES), :]
        bg = b_buf[pl.ds(r0, SUBLANES), :]
        for k in (1, 2, 4):
            keep = row >= k
            a_sh = jnp.where(keep, pltpu.roll(ag, k, 0), 1.0)
            b_sh = jnp.where(keep, pltpu.roll(bg, k, 0), 0.0)
            bg = ag * b_sh + bg
            ag = ag * a_sh
        hg = ag * carry + bg
        h_buf[pl.ds(r0, SUBLANES), :] = hg
        return jnp.broadcast_to(hg[SUBLANES - 1:SUBLANES, :], (SUBLANES, LRU_WIDTH))

    hcarry[...] = lax.fori_loop(0, tm // SUBLANES, group, hcarry[...], unroll=2)

    y_main = h_buf[...] * _gelu_tanh(u_gate)
    y_mem = _memory_attention(q_mem, mk_ref, mv_ref)
    o_ref[...] = _mix_out(x, y_main, y_mem, wout_ref, gpost_ref)


def _rglru_layer(x, layer, j, p):
    bsz, seq, _ = x.shape
    tm = TM_MIX
    n_in = 2 * LRU_WIDTH + MEM_WIDTH
    const3 = lambda b, t: (0, 0, 0)
    in_specs = [
        pl.BlockSpec((None, tm, D_MODEL), lambda b, t: (b, t, 0)),
        pl.BlockSpec((None, 1, D_MODEL), lambda b, t: (layer, 0, 0)),
        pl.BlockSpec((None, D_MODEL, n_in), lambda b, t: (j, 0, 0)),
        pl.BlockSpec((None, LRU_CONV, LRU_WIDTH), lambda b, t: (j, 0, 0)),
        pl.BlockSpec((None, 1, LRU_WIDTH), lambda b, t: (j, 0, 0)),
        pl.BlockSpec((None, GATE_CHUNKS, MXU_DIM, 2 * MXU_DIM), lambda b, t: (j, 0, 0, 0)),
        pl.BlockSpec((None, GATE_CHUNKS, 1, 2 * MXU_DIM), lambda b, t: (j, 0, 0, 0)),
        pl.BlockSpec((None, 1, LRU_WIDTH), lambda b, t: (j, 0, 0)),
        pl.BlockSpec((None, None, MEM_LEN, MEM_WIDTH), lambda b, t: (layer, b, 0, 0)),
        pl.BlockSpec((None, None, MEM_LEN, MEM_WIDTH), lambda b, t: (layer, b, 0, 1)),
        pl.BlockSpec((None, D_MODEL, D_MODEL), lambda b, t: (layer, 0, 0)),
        pl.BlockSpec((None, 1, D_MODEL), lambda b, t: (layer, 0, 0)),
    ]
    del const3
    return pl.pallas_call(
        _rglru_kernel,
        grid=(bsz, seq // tm),
        in_specs=in_specs,
        out_specs=pl.BlockSpec((None, tm, D_MODEL), lambda b, t: (b, t, 0)),
        out_shape=jax.ShapeDtypeStruct(x.shape, F32),
        scratch_shapes=[
            pltpu.VMEM((SUBLANES + tm, LRU_WIDTH), F32),
            pltpu.VMEM((tm, LRU_WIDTH), F32),
            pltpu.VMEM((tm, LRU_WIDTH), F32),
            pltpu.VMEM((tm, LRU_WIDTH), F32),
            pltpu.VMEM((SUBLANES, LRU_WIDTH), F32),
        ],
        compiler_params=pltpu.CompilerParams(
            dimension_semantics=("arbitrary", "arbitrary"), vmem_limit_bytes=VMEM_LIMIT_BYTES),
        name=f"rglru_mixer_{layer}",
    )(x, p["g_mix_pre"], p["w_in_a"], p["w_conv_a"], p["b_conv_a"], p["w_gate"], p["b_gate"],
      p["lru_lambda"], p["mkv"], p["mkv"], p["w_mix_out"], p["g_mix_post"])


def _swa_kernel(sinks_ref, x_ref, gpre_ref, win_ref, kvp_ref, kvc_ref, mk_ref, mv_ref,
                wout_ref, gpost_ref, o_ref, kv_buf, y_buf):
    tm = x_ref.shape[0]
    nblk = tm // WINDOW
    first_tile = pl.program_id(1) == 0

    x = x_ref[...]
    h = _rmsnorm(x, gpre_ref[...]).astype(BF16)
    proj = _dot(h, win_ref[...])
    q_mem = proj[:, MIX_WIDTH:]

    kv_buf[0:WINDOW, :] = kvp_ref[...]
    kv_buf[WINDOW:, :] = kvc_ref[...]

    qi = lax.broadcasted_iota(jnp.int32, (WINDOW, 2 * WINDOW), 0)
    kj = lax.broadcasted_iota(jnp.int32, (WINDOW, 2 * WINDOW), 1)
    dist = qi + WINDOW - kj
    in_window = (dist >= 0) & (dist < WINDOW)
    dist_f = dist.astype(F32)
    neg_inf = jnp.float32(-jnp.inf)
    no_prev = jnp.where(kj < jnp.where(first_tile, WINDOW, 0), neg_inf, 0.0)

    for g in range(SWA_KV_HEADS):
        heads = range(g * SWA_GROUP, (g + 1) * SWA_GROUP)
        bias = jnp.concatenate(
            [jnp.where(in_window, -SLOPES[hd] * dist_f, neg_inf) for hd in heads], axis=0)
        sink = jnp.concatenate(
            [jnp.full((WINDOW, 1), sinks_ref[hd], F32) for hd in heads], axis=0)
        ksl = slice(g * HEAD_DIM, (g + 1) * HEAD_DIM)
        vsl = slice(KV_WIDTH + g * HEAD_DIM, KV_WIDTH + (g + 1) * HEAD_DIM)
        for qb in range(nblk):
            rows = slice(qb * WINDOW, (qb + 1) * WINDOW)
            band = slice(qb * WINDOW, (qb + 2) * WINDOW)
            q = jnp.concatenate(
                [proj[rows, hd * HEAD_DIM:(hd + 1) * HEAD_DIM] for hd in heads], axis=0)
            q = (q * (HEAD_DIM ** -0.5)).astype(BF16)
            s = _dot_nt(q, kv_buf[band, ksl]) + bias
            if qb == 0:
                s = s + jnp.concatenate([no_prev] * SWA_GROUP, axis=0)
            m = jnp.maximum(jnp.max(s, axis=-1, keepdims=True), sink)
            p = jnp.exp(s - m)
            denom = jnp.sum(p, axis=-1, keepdims=True) + jnp.exp(sink - m)
            p = p * (1.0 / denom)
            o = _dot(p.astype(BF16), kv_buf[band, vsl])
            for n, hd in enumerate(heads):
                y_buf[rows, hd * HEAD_DIM:(hd + 1) * HEAD_DIM] = o[n * WINDOW:(n + 1) * WINDOW, :]

    y_mem = _memory_attention(q_mem, mk_ref, mv_ref)
    o_ref[...] = _mix_out(x, y_buf[...], y_mem, wout_ref, gpost_ref)


def _swa_layer(x, kv, layer, j, p):
    bsz, seq, _ = x.shape
    tm = TM_MIX
    per = tm // WINDOW
    in_specs = [
        pl.BlockSpec(memory_space=pltpu.SMEM),
        pl.BlockSpec((None, tm, D_MODEL), lambda b, t: (b, t, 0)),
        pl.BlockSpec((None, 1, D_MODEL), lambda b, t: (layer, 0, 0)),
        pl.BlockSpec((None, D_MODEL, D_MODEL), lambda b, t: (j, 0, 0)),
        pl.BlockSpec((None, WINDOW, 2 * KV_WIDTH),
                     lambda b, t: (b, jnp.maximum(t * per - 1, 0), 0)),
        pl.BlockSpec((None, tm, 2 * KV_WIDTH), lambda b, t: (b, t, 0)),
        pl.BlockSpec((None, None, MEM_LEN, MEM_WIDTH), lambda b, t: (layer, b, 0, 0)),
        pl.BlockSpec((None, None, MEM_LEN, MEM_WIDTH), lambda b, t: (layer, b, 0, 1)),
        pl.BlockSpec((None, D_MODEL, D_MODEL), lambda b, t: (layer, 0, 0)),
        pl.BlockSpec((None, 1, D_MODEL), lambda b, t: (layer, 0, 0)),
    ]
    return pl.pallas_call(
        _swa_kernel,
        grid=(bsz, seq // tm),
        in_specs=in_specs,
        out_specs=pl.BlockSpec((None, tm, D_MODEL), lambda b, t: (b, t, 0)),
        out_shape=jax.ShapeDtypeStruct(x.shape, F32),
        scratch_shapes=[
            pltpu.VMEM((WINDOW + tm, 2 * KV_WIDTH), BF16),
            pltpu.VMEM((tm, MIX_WIDTH), F32),
        ],
        compiler_params=pltpu.CompilerParams(
            dimension_semantics=("arbitrary", "arbitrary"), vmem_limit_bytes=VMEM_LIMIT_BYTES),
        name=f"swa_mixer_{layer}",
    )(p["sinks_b"][j], x, p["g_mix_pre"], p["w_in_b"], kv, kv, p["mkv"], p["mkv"],
      p["w_mix_out"], p["g_mix_post"])


def _ffn_kernel(x_ref, gpre_ref, wup_ref, wconv_ref, bconv_ref, wdown_ref, gpost_ref, o_ref,
                ubuf, tail, acc):
    tm = x_ref.shape[0]
    pad = SUBLANES

    @pl.when(pl.program_id(1) == 0)
    def _():
        tail[...] = jnp.zeros_like(tail)

    x = x_ref[...]
    h = _rmsnorm(x, gpre_ref[...]).astype(BF16)

    def conv(slot, col0):
        cols = slice(col0, col0 + FFN_CHUNK)
        u = _dot(h, wup_ref[:, cols])
        ubuf[slot, 0:pad, :] = tail[:, cols]
        ubuf[slot, pad:pad + tm, :] = u
        tail[:, cols] = u[tm - pad:, :]
        out = u * wconv_ref[FFN_CONV - 1:FFN_CONV, cols] + bconv_ref[:, cols]
        for k in range(FFN_CONV - 1):
            shift = FFN_CONV - 1 - k
            out = out + ubuf[slot, pad - shift:pad - shift + tm, :] * wconv_ref[k:k + 1, cols]
        return out

    for c in range(FFN_CHUNKS):
        gate = conv(2 * (c % 2), c * FFN_CHUNK)
        val = conv(2 * (c % 2) + 1, D_FF + c * FFN_CHUNK)
        act = (_gelu_tanh(gate) * val).astype(BF16)
        part = _dot(act, wdown_ref[c * FFN_CHUNK:(c + 1) * FFN_CHUNK, :])
        if c == 0:
            acc[...] = part
        else:
            acc[...] += part

    o_ref[...] = x + _rmsnorm(acc[...], gpost_ref[...])


def _ffn_layer(x, layer, p):
    bsz, seq, _ = x.shape
    tm = TM_FFN
    in_specs = [
        pl.BlockSpec((None, tm, D_MODEL), lambda b, t: (b, t, 0)),
        pl.BlockSpec((None, 1, D_MODEL), lambda b, t: (layer, 0, 0)),
        pl.BlockSpec((None, D_MODEL, 2 * D_FF), lambda b, t: (layer, 0, 0)),
        pl.BlockSpec((None, FFN_CONV, 2 * D_FF), lambda b, t: (layer, 0, 0)),
        pl.BlockSpec((None, 1, 2 * D_FF), lambda b, t: (layer, 0, 0)),
        pl.BlockSpec((None, D_FF, D_MODEL), lambda b, t: (layer, 0, 0)),
        pl.BlockSpec((None, 1, D_MODEL), lambda b, t: (layer, 0, 0)),
    ]
    return pl.pallas_call(
        _ffn_kernel,
        grid=(bsz, seq // tm),
        in_specs=in_specs,
        out_specs=pl.BlockSpec((None, tm, D_MODEL), lambda b, t: (b, t, 0)),
        out_shape=jax.ShapeDtypeStruct(x.shape, F32),
        scratch_shapes=[
            pltpu.VMEM((4, SUBLANES + tm, FFN_CHUNK), F32),
            pltpu.VMEM((SUBLANES, 2 * D_FF), F32),
            pltpu.VMEM((tm, D_MODEL), F32),
        ],
        compiler_params=pltpu.CompilerParams(
            dimension_semantics=("arbitrary", "arbitrary"), vmem_limit_bytes=VMEM_LIMIT_BYTES),
        name=f"ffn_{layer}",
    )(x, p["g_ffn_pre"], p["w_ffn_up"], p["w_ffn_conv"], p["b_ffn_conv"], p["w_ffn_down"],
      p["g_ffn_post"])


def _block_diag_gates(w_r, w_i, b_r, b_i):
    na = w_r.shape[0]
    per = MXU_DIM // LRU_BLOCK

    def bd(w):
        w = w.reshape(na, GATE_CHUNKS, per, LRU_BLOCK, LRU_BLOCK)
        eye = jnp.eye(per, dtype=w.dtype)
        full = w[:, :, :, :, None, :] * eye[None, None, :, None, :, None]
        return full.reshape(na, GATE_CHUNKS, MXU_DIM, MXU_DIM)

    w = jnp.concatenate([bd(w_r), bd(w_i)], axis=-1).astype(BF16)
    b = jnp.concatenate([b_r.reshape(na, GATE_CHUNKS, 1, MXU_DIM),
                         b_i.reshape(na, GATE_CHUNKS, 1, MXU_DIM)], axis=-1)
    return w, b


@jax.jit
def kernel(x, mem, g_mix_pre, g_mix_post, g_ffn_pre, g_ffn_post, g_mem, w_mem_kv, w_mix_out,
           w_ffn_up, w_ffn_conv, b_ffn_conv, w_ffn_down, w_in_a, w_conv_a, b_conv_a,
           w_rg_r, b_rg_r, w_rg_i, b_rg_i, lru_lambda, w_in_b, sinks_b, g_kv, w_kv):
    row = lambda a: a.reshape(a.shape[0], 1, a.shape[-1])
    w_gate, b_gate = _block_diag_gates(w_rg_r, w_rg_i, b_rg_r, b_rg_i)
    p = {
        "g_mix_pre": row(g_mix_pre), "g_mix_post": row(g_mix_post),
        "g_ffn_pre": row(g_ffn_pre), "g_ffn_post": row(g_ffn_post),
        "w_mix_out": w_mix_out.astype(BF16),
        "w_ffn_up": w_ffn_up.astype(BF16), "w_ffn_conv": w_ffn_conv,
        "b_ffn_conv": row(b_ffn_conv), "w_ffn_down": w_ffn_down.astype(BF16),
        "w_in_a": w_in_a.astype(BF16), "w_conv_a": w_conv_a, "b_conv_a": row(b_conv_a),
        "w_gate": w_gate, "b_gate": b_gate, "lru_lambda": row(lru_lambda),
        "w_in_b": w_in_b.astype(BF16), "sinks_b": sinks_b,
    }
    p["mkv"] = _memkv(mem, row(g_mem), w_mem_kv.astype(BF16))
    kv = None
    for layer in range(DEPTH):
        if layer < N_A_LAYERS:
            x = _rglru_layer(x, layer, layer, p)
        else:
            if layer == N_A_LAYERS:
                kv = _shared_kv(x, g_kv.reshape(1, D_MODEL), w_kv.astype(BF16))
            x = _swa_layer(x, kv, layer, layer - N_A_LAYERS, p)
        x = _ffn_layer(x, layer, p)
    return x
```

```python
import math

import jax
import jax.numpy as jnp
import numpy as np
from jax import lax
from jax.experimental import pallas as pl
from jax.experimental.pallas import tpu as pltpu

D_MODEL = 1024
DEPTH = 4
N_A_LAYERS = DEPTH // 2
HEAD_DIM = 64
MEM_LEN = 256
MEM_HEADS = 4
MEM_WIDTH = MEM_HEADS * HEAD_DIM
MIX_WIDTH = D_MODEL - MEM_WIDTH
LRU_WIDTH = MIX_WIDTH
LRU_BLOCK = 64
LRU_CONV = 4
LRU_C = 8.0
SWA_HEADS = MIX_WIDTH // HEAD_DIM
SWA_KV_HEADS = 4
SWA_GROUP = SWA_HEADS // SWA_KV_HEADS
KV_WIDTH = SWA_KV_HEADS * HEAD_DIM
WINDOW = 128
D_FF = 2816
FFN_CONV = 3
EPS = 1e-6

SUBLANES = 8
LANES = 128
MXU_DIM = 256
VMEM_LIMIT_BYTES = 56 * 1024 * 1024

HEADS_PER_TILE = LANES // HEAD_DIM
KPAD_WIDTH = 2 * SWA_KV_HEADS * LANES
MKPAD_WIDTH = MEM_HEADS * LANES

GATE_CHUNKS = LRU_WIDTH // MXU_DIM
FFN_CHUNK = 256
FFN_CHUNKS = D_FF // FFN_CHUNK

TM_MIX = 256
TM_FFN = 256
TM_KV = 512

BF16 = jnp.bfloat16
F32 = jnp.float32
QK_SCALE = HEAD_DIM ** -0.5


def _alibi_slopes(n):
    def pow2_slopes(m):
        start = 2.0 ** (-8.0 / m)
        return [start ** (i + 1) for i in range(m)]
    c = 2 ** int(math.floor(math.log2(n)))
    s = pow2_slopes(c)
    if c != n:
        s = s + pow2_slopes(2 * c)[0::2][: n - c]
    return [float(np.float32(v)) for v in s]


SLOPES = _alibi_slopes(SWA_HEADS)


def _rmsnorm(x, g):
    ms = jnp.mean(x * x, axis=-1, keepdims=True)
    return x * lax.rsqrt(ms + EPS) * g


def _gelu_tanh(x):
    c = math.sqrt(2.0 / math.pi)
    return 0.5 * x * (1.0 + jnp.tanh(c * (x + 0.044715 * (x * x * x))))


def _dot(a, b):
    return jnp.dot(a, b, preferred_element_type=F32)


def _dot_nt(a, b):
    return lax.dot_general(a, b, (((1,), (1,)), ((), ())), preferred_element_type=F32)


def _dot_tn(a, b):
    return lax.dot_general(a, b, (((0,), (0,)), ((), ())), preferred_element_type=F32)


def _half_tiles(k, parities):
    upper = lax.broadcasted_iota(jnp.int32, (k.shape[0], LANES), 1) >= HEAD_DIM
    tiles = []
    for i, par in enumerate(parities):
        t = k[:, i * LANES:(i + 1) * LANES]
        tiles.append(jnp.where(upper if par else ~upper, t, 0.0))
    return jnp.concatenate(tiles, axis=-1).astype(BF16)


def _softmax_t(s_t, extra=None):
    m = jnp.max(s_t, axis=0, keepdims=True)
    if extra is not None:
        m = jnp.maximum(m, extra)
    p = jnp.exp(s_t - m)
    d = jnp.sum(p, axis=0, keepdims=True)
    if extra is not None:
        d = d + jnp.exp(extra - m)
    return (p * (1.0 / d)).astype(BF16)


def _memory_attention_t(q16, tile0, mk_ref, mvt_ref, yt_ref, row0):
    s_ts = []
    for hh in range(MEM_HEADS):
        tile = tile0 + hh // HEADS_PER_TILE
        s_ts.append(_dot_nt(mk_ref[:, hh * LANES:(hh + 1) * LANES],
                            q16[:, tile * LANES:(tile + 1) * LANES]))
    p_ts = [_softmax_t(s_t) for s_t in s_ts]
    for hh in range(MEM_HEADS):
        o_t = _dot(mvt_ref[hh * HEAD_DIM:(hh + 1) * HEAD_DIM, :], p_ts[hh])
        yt_ref[row0 + hh * HEAD_DIM:row0 + (hh + 1) * HEAD_DIM, :] = o_t.astype(BF16)


def _memkv_kernel(mem_ref, g_ref, wk_ref, wvt_ref, mk_ref, mvt_ref):
    m = mem_ref[...]
    ms = jnp.mean(m * m, axis=-1, keepdims=True)
    mn = m * lax.rsqrt(ms + EPS)
    for l in range(DEPTH):
        mnl = (mn * g_ref[l]).astype(BF16)
        k = _dot(mnl, wk_ref[l])
        k4 = jnp.concatenate([k[:, (hh // 2) * LANES:(hh // 2 + 1) * LANES]
                              for hh in range(MEM_HEADS)], axis=-1)
        mk_ref[l] = _half_tiles(k4, [hh % 2 for hh in range(MEM_HEADS)])
        mvt_ref[l] = _dot_nt(wvt_ref[l], mnl).astype(BF16)


def _memkv(mem, g_mem, wk_b, wvt_b):
    bsz = mem.shape[0]
    return pl.pallas_call(
        _memkv_kernel,
        grid=(bsz,),
        in_specs=[
            pl.BlockSpec((None, MEM_LEN, D_MODEL), lambda b: (b, 0, 0)),
            pl.BlockSpec((DEPTH, 1, D_MODEL), lambda b: (0, 0, 0)),
            pl.BlockSpec((DEPTH, D_MODEL, MEM_WIDTH), lambda b: (0, 0, 0)),
            pl.BlockSpec((DEPTH, MEM_WIDTH, D_MODEL), lambda b: (0, 0, 0)),
        ],
        out_specs=[
            pl.BlockSpec((DEPTH, None, MEM_LEN, MKPAD_WIDTH), lambda b: (0, b, 0, 0)),
            pl.BlockSpec((DEPTH, None, MEM_WIDTH, MEM_LEN), lambda b: (0, b, 0, 0)),
        ],
        out_shape=[
            jax.ShapeDtypeStruct((DEPTH, bsz, MEM_LEN, MKPAD_WIDTH), BF16),
            jax.ShapeDtypeStruct((DEPTH, bsz, MEM_WIDTH, MEM_LEN), BF16),
        ],
        compiler_params=pltpu.CompilerParams(dimension_semantics=("arbitrary",)),
        name="memkv",
    )(mem, g_mem, wk_b, wvt_b)


def _kv_kernel(x_ref, g_ref, wk2_ref, wvt_ref, kpad_ref, vt_ref):
    h = _rmsnorm(x_ref[...], g_ref[...]).astype(BF16)
    k2 = _dot(h, wk2_ref[...])
    nat = [k2[:, i * LANES:(i + 1) * LANES] for i in range(2)]
    swp = [k2[:, KV_WIDTH + i * LANES:KV_WIDTH + (i + 1) * LANES] for i in range(2)]
    lo = jnp.concatenate([nat[0], swp[0], nat[1], swp[1]], axis=-1)
    hi = jnp.concatenate([swp[0], nat[0], swp[1], nat[1]], axis=-1)
    kpad_ref[:, :SWA_KV_HEADS * LANES] = _half_tiles(lo, [0] * SWA_KV_HEADS)
    kpad_ref[:, SWA_KV_HEADS * LANES:] = _half_tiles(hi, [1] * SWA_KV_HEADS)
    vt_ref[...] = _dot_nt(wvt_ref[...], h).astype(BF16)


def _shared_kv(x, g_kv, wk2_b, wvt_b):
    bsz, seq, _ = x.shape
    return pl.pallas_call(
        _kv_kernel,
        grid=(bsz, seq // TM_KV),
        in_specs=[
            pl.BlockSpec((None, TM_KV, D_MODEL), lambda b, t: (b, t, 0)),
            pl.BlockSpec((1, D_MODEL), lambda b, t: (0, 0)),
            pl.BlockSpec((D_MODEL, 2 * KV_WIDTH), lambda b, t: (0, 0)),
            pl.BlockSpec((KV_WIDTH, D_MODEL), lambda b, t: (0, 0)),
        ],
        out_specs=[
            pl.BlockSpec((None, TM_KV, KPAD_WIDTH), lambda b, t: (b, t, 0)),
            pl.BlockSpec((None, KV_WIDTH, TM_KV), lambda b, t: (b, 0, t)),
        ],
        out_shape=[
            jax.ShapeDtypeStruct((bsz, seq, KPAD_WIDTH), BF16),
            jax.ShapeDtypeStruct((bsz, KV_WIDTH, seq), BF16),
        ],
        compiler_params=pltpu.CompilerParams(dimension_semantics=("arbitrary", "arbitrary")),
        name="shared_kv",
    )(x, g_kv, wk2_b, wvt_b)


def _rglru_kernel(x_ref, gpre_ref, win_ref, wconv_ref, bconv_ref, wgate_ref, bgate_ref,
                  lam_ref, mk_ref, mvt_ref, wout_ref, gpost_ref, o_ref,
                  ubuf, a_buf, b_buf, h_buf, hcarry, ymt_buf):
    tm = x_ref.shape[0]
    pad = SUBLANES

    @pl.when(pl.program_id(1) == 0)
    def _():
        ubuf[0:pad, :] = jnp.zeros((pad, LRU_WIDTH), F32)
        hcarry[...] = jnp.zeros_like(hcarry)

    x = x_ref[...]
    h = _rmsnorm(x, gpre_ref[...]).astype(BF16)
    proj = _dot(h, win_ref[...])
    u_gate = proj[:, :LRU_WIDTH]
    u_x = proj[:, LRU_WIDTH:2 * LRU_WIDTH]
    q16 = (proj[:, 2 * LRU_WIDTH:] * QK_SCALE).astype(BF16)

    _memory_attention_t(q16, 0, mk_ref, mvt_ref, ymt_buf, 0)

    ubuf[pad:pad + tm, :] = u_x
    xc = u_x * wconv_ref[LRU_CONV - 1:LRU_CONV, :] + bconv_ref[...]
    for k in range(LRU_CONV - 1):
        shift = LRU_CONV - 1 - k
        xc = xc + ubuf[pad - shift:pad - shift + tm, :] * wconv_ref[k:k + 1, :]
    ubuf[0:pad, :] = u_x[tm - pad:, :]

    xcb = xc.astype(BF16)
    rs, is_ = [], []
    for c in range(GATE_CHUNKS):
        g = _dot(xcb[:, c * MXU_DIM:(c + 1) * MXU_DIM], wgate_ref[c]) + bgate_ref[c]
        rs.append(jax.nn.sigmoid(g[:, :MXU_DIM]))
        is_.append(jax.nn.sigmoid(g[:, MXU_DIM:]))
    r = jnp.concatenate(rs, axis=-1)
    i = jnp.concatenate(is_, axis=-1)

    lam = lam_ref[...]
    softplus_neg_lam = jnp.maximum(-lam, 0.0) + jnp.log1p(jnp.exp(-jnp.abs(lam)))
    log_a = (-LRU_C) * r * softplus_neg_lam
    a = jnp.exp(log_a)
    gain = jnp.sqrt(-jnp.tanh(log_a) * (a * a + 1.0))
    a_buf[...] = a
    b_buf[...] = gain * (i * xc)

    row = lax.broadcasted_iota(jnp.int32, (SUBLANES, LRU_WIDTH), 0)

    def group(gi, carry):
        r0 = pl.multiple_of(gi * SUBLANES, SUBLANES)
        ag = a_buf[pl.ds(r0, SUBLANES), :]
        bg = b_buf[pl.ds(r0, SUBLANES), :]
        for k in (1, 2, 4):
            keep = row >= k
            a_sh = jnp.where(keep, pltpu.roll(ag, k, 0), 1.0)
            b_sh = jnp.where(keep, pltpu.roll(bg, k, 0), 0.0)
            bg = ag * b_sh + bg
            ag = ag * a_sh
        hg = ag * carry + bg
        h_buf[pl.ds(r0, SUBLANES), :] = hg
        return jnp.broadcast_to(hg[SUBLANES - 1:SUBLANES, :], (SUBLANES, LRU_WIDTH))

    hcarry[...] = lax.fori_loop(0, tm // SUBLANES, group, hcarry[...], unroll=2)

    y_main = (h_buf[...] * _gelu_tanh(u_gate)).astype(BF16)
    y = _dot(y_main, wout_ref[:MIX_WIDTH, :]) + _dot_tn(ymt_buf[...], wout_ref[MIX_WIDTH:, :])
    o_ref[...] = x + _rmsnorm(y, gpost_ref[...])


def _rglru_layer(x, layer, j, p):
    bsz, seq, _ = x.shape
    tm = TM_MIX
    n_in = 2 * LRU_WIDTH + MEM_WIDTH
    in_specs = [
        pl.BlockSpec((None, tm, D_MODEL), lambda b, t: (b, t, 0)),
        pl.BlockSpec((None, 1, D_MODEL), lambda b, t: (layer, 0, 0)),
        pl.BlockSpec((None, D_MODEL, n_in), lambda b, t: (j, 0, 0)),
        pl.BlockSpec((None, LRU_CONV, LRU_WIDTH), lambda b, t: (j, 0, 0)),
        pl.BlockSpec((None, 1, LRU_WIDTH), lambda b, t: (j, 0, 0)),
        pl.BlockSpec((None, GATE_CHUNKS, MXU_DIM, 2 * MXU_DIM), lambda b, t: (j, 0, 0, 0)),
        pl.BlockSpec((None, GATE_CHUNKS, 1, 2 * MXU_DIM), lambda b, t: (j, 0, 0, 0)),
        pl.BlockSpec((None, 1, LRU_WIDTH), lambda b, t: (j, 0, 0)),
        pl.BlockSpec((None, None, MEM_LEN, MKPAD_WIDTH), lambda b, t: (layer, b, 0, 0)),
        pl.BlockSpec((None, None, MEM_WIDTH, MEM_LEN), lambda b, t: (layer, b, 0, 0)),
        pl.BlockSpec((None, D_MODEL, D_MODEL), lambda b, t: (layer, 0, 0)),
        pl.BlockSpec((None, 1, D_MODEL), lambda b, t: (layer, 0, 0)),
    ]
    return pl.pallas_call(
        _rglru_kernel,
        grid=(bsz, seq // tm),
        in_specs=in_specs,
        out_specs=pl.BlockSpec((None, tm, D_MODEL), lambda b, t: (b, t, 0)),
        out_shape=jax.ShapeDtypeStruct(x.shape, F32),
        scratch_shapes=[
            pltpu.VMEM((SUBLANES + tm, LRU_WIDTH), F32),
            pltpu.VMEM((tm, LRU_WIDTH), F32),
            pltpu.VMEM((tm, LRU_WIDTH), F32),
            pltpu.VMEM((tm, LRU_WIDTH), F32),
            pltpu.VMEM((SUBLANES, LRU_WIDTH), F32),
            pltpu.VMEM((MEM_WIDTH, tm), BF16),
        ],
        compiler_params=pltpu.CompilerParams(
            dimension_semantics=("arbitrary", "arbitrary"), vmem_limit_bytes=VMEM_LIMIT_BYTES),
        name=f"rglru_mixer_{layer}",
    )(x, p["g_mix_pre"], p["w_in_a"], p["w_conv_a"], p["b_conv_a"], p["w_gate"], p["b_gate"],
      p["lru_lambda"], p["mk_pad"], p["mv_t"], p["w_mix_out"], p["g_mix_post"])


def _swa_kernel(sinks_ref, x_ref, gpre_ref, win_ref, kp_ref, kc_ref, vtp_ref, vtc_ref,
                mk_ref, mvt_ref, wout_ref, gpost_ref, o_ref, k_buf, vt_buf, bias_buf, yt_buf):
    tm = x_ref.shape[0]
    nblk = tm // WINDOW
    band_keys = 2 * WINDOW

    @pl.when((pl.program_id(0) == 0) & (pl.program_id(1) == 0))
    def _():
        kj = lax.broadcasted_iota(jnp.int32, (band_keys, WINDOW), 0)
        qi = lax.broadcasted_iota(jnp.int32, (band_keys, WINDOW), 1)
        dist = qi + WINDOW - kj
        in_window = (dist >= 0) & (dist < WINDOW)
        dist_f = dist.astype(F32)
        for hd in range(SWA_HEADS):
            bias_buf[hd] = jnp.where(in_window, -SLOPES[hd] * dist_f, -jnp.inf)

    x = x_ref[...]
    h = _rmsnorm(x, gpre_ref[...]).astype(BF16)
    q16 = (_dot(h, win_ref[...]) * QK_SCALE).astype(BF16)

    k_buf[0:WINDOW, :] = kp_ref[...]
    k_buf[WINDOW:, :] = kc_ref[...]
    vt_buf[:, 0:WINDOW] = vtp_ref[...]
    vt_buf[:, WINDOW:] = vtc_ref[...]

    kj = lax.broadcasted_iota(jnp.int32, (band_keys, WINDOW), 0)
    no_prev = jnp.where(kj < jnp.where(pl.program_id(1) == 0, WINDOW, 0), -jnp.inf, 0.0)

    units = [(hd, qb) for hd in range(SWA_HEADS) for qb in range(nblk)]
    s_ts = []
    for hd, qb in units:
        g = hd // SWA_GROUP
        ktile = (hd % HEADS_PER_TILE) * SWA_KV_HEADS + g
        qtile = hd // HEADS_PER_TILE
        s_ts.append(_dot_nt(
            k_buf[qb * WINDOW:qb * WINDOW + band_keys, ktile * LANES:(ktile + 1) * LANES],
            q16[qb * WINDOW:(qb + 1) * WINDOW, qtile * LANES:(qtile + 1) * LANES]))

    _memory_attention_t(q16, MIX_WIDTH // LANES, mk_ref, mvt_ref, yt_buf, MIX_WIDTH)

    p_ts = []
    for (hd, qb), s_t in zip(units, s_ts):
        s_t = s_t + bias_buf[hd]
        if qb == 0:
            s_t = s_t + no_prev
        p_ts.append(_softmax_t(s_t, jnp.full((1, WINDOW), sinks_ref[hd], F32)))

    for (hd, qb), p_t in zip(units, p_ts):
        g = hd // SWA_GROUP
        o_t = _dot(vt_buf[g * HEAD_DIM:(g + 1) * HEAD_DIM, qb * WINDOW:qb * WINDOW + band_keys], p_t)
        yt_buf[hd * HEAD_DIM:(hd + 1) * HEAD_DIM, qb * WINDOW:(qb + 1) * WINDOW] = o_t.astype(BF16)

    y = _dot_tn(yt_buf[...], wout_ref[...])
    o_ref[...] = x + _rmsnorm(y, gpost_ref[...])


def _swa_layer(x, kpad, vt, layer, j, p):
    bsz, seq, _ = x.shape
    tm = TM_MIX
    per = tm // WINDOW
    prev_blk = lambda t: jnp.maximum(t * per - 1, 0)
    in_specs = [
        pl.BlockSpec(memory_space=pltpu.SMEM),
        pl.BlockSpec((None, tm, D_MODEL), lambda b, t: (b, t, 0)),
        pl.BlockSpec((None, 1, D_MODEL), lambda b, t: (layer, 0, 0)),
        pl.BlockSpec((None, D_MODEL, D_MODEL), lambda b, t: (j, 0, 0)),
        pl.BlockSpec((None, WINDOW, KPAD_WIDTH), lambda b, t: (b, prev_blk(t), 0)),
        pl.BlockSpec((None, tm, KPAD_WIDTH), lambda b, t: (b, t, 0)),
        pl.BlockSpec((None, KV_WIDTH, WINDOW), lambda b, t: (b, 0, prev_blk(t))),
        pl.BlockSpec((None, KV_WIDTH, tm), lambda b, t: (b, 0, t)),
        pl.BlockSpec((None, None, MEM_LEN, MKPAD_WIDTH), lambda b, t: (layer, b, 0, 0)),
        pl.BlockSpec((None, None, MEM_WIDTH, MEM_LEN), lambda b, t: (layer, b, 0, 0)),
        pl.BlockSpec((None, D_MODEL, D_MODEL), lambda b, t: (layer, 0, 0)),
        pl.BlockSpec((None, 1, D_MODEL), lambda b, t: (layer, 0, 0)),
    ]
    return pl.pallas_call(
        _swa_kernel,
        grid=(bsz, seq // tm),
        in_specs=in_specs,
        out_specs=pl.BlockSpec((None, tm, D_MODEL), lambda b, t: (b, t, 0)),
        out_shape=jax.ShapeDtypeStruct(x.shape, F32),
        scratch_shapes=[
            pltpu.VMEM((WINDOW + tm, KPAD_WIDTH), BF16),
            pltpu.VMEM((KV_WIDTH, WINDOW + tm), BF16),
            pltpu.VMEM((SWA_HEADS, 2 * WINDOW, WINDOW), F32),
            pltpu.VMEM((D_MODEL, tm), BF16),
        ],
        compiler_params=pltpu.CompilerParams(
            dimension_semantics=("arbitrary", "arbitrary"), vmem_limit_bytes=VMEM_LIMIT_BYTES),
        name=f"swa_mixer_{layer}",
    )(p["sinks_b"][j], x, p["g_mix_pre"], p["w_in_b"], kpad, kpad, vt, vt, p["mk_pad"], p["mv_t"],
      p["w_mix_out"], p["g_mix_post"])


def _ffn_kernel(x_ref, gpre_ref, wup_ref, wconv_ref, bconv_ref, wdown_ref, gpost_ref, o_ref,
                ubuf, tail, acc):
    tm = x_ref.shape[0]
    pad = SUBLANES

    @pl.when(pl.program_id(1) == 0)
    def _():
        tail[...] = jnp.zeros_like(tail)

    x = x_ref[...]
    h = _rmsnorm(x, gpre_ref[...]).astype(BF16)

    def conv(slot, col0):
        cols = slice(col0, col0 + FFN_CHUNK)
        u = _dot(h, wup_ref[:, cols])
        ubuf[slot, 0:pad, :] = tail[:, cols]
        ubuf[slot, pad:pad + tm, :] = u
        tail[:, cols] = u[tm - pad:, :]
        out = u * wconv_ref[FFN_CONV - 1:FFN_CONV, cols] + bconv_ref[:, cols]
        for k in range(FFN_CONV - 1):
            shift = FFN_CONV - 1 - k
            out = out + ubuf[slot, pad - shift:pad - shift + tm, :] * wconv_ref[k:k + 1, cols]
        return out

    for c in range(FFN_CHUNKS):
        gate = conv(2 * (c % 2), c * FFN_CHUNK)
        val = conv(2 * (c % 2) + 1, D_FF + c * FFN_CHUNK)
        act = (_gelu_tanh(gate) * val).astype(BF16)
        part = _dot(act, wdown_ref[c * FFN_CHUNK:(c + 1) * FFN_CHUNK, :])
        if c == 0:
            acc[...] = part
        else:
            acc[...] += part

    o_ref[...] = x + _rmsnorm(acc[...], gpost_ref[...])


def _ffn_layer(x, layer, p):
    bsz, seq, _ = x.shape
    tm = TM_FFN
    in_specs = [
        pl.BlockSpec((None, tm, D_MODEL), lambda b, t: (b, t, 0)),
        pl.BlockSpec((None, 1, D_MODEL), lambda b, t: (layer, 0, 0)),
        pl.BlockSpec((None, D_MODEL, 2 * D_FF), lambda b, t: (layer, 0, 0)),
        pl.BlockSpec((None, FFN_CONV, 2 * D_FF), lambda b, t: (layer, 0, 0)),
        pl.BlockSpec((None, 1, 2 * D_FF), lambda b, t: (layer, 0, 0)),
        pl.BlockSpec((None, D_FF, D_MODEL), lambda b, t: (layer, 0, 0)),
        pl.BlockSpec((None, 1, D_MODEL), lambda b, t: (layer, 0, 0)),
    ]
    return pl.pallas_call(
        _ffn_kernel,
        grid=(bsz, seq // tm),
        in_specs=in_specs,
        out_specs=pl.BlockSpec((None, tm, D_MODEL), lambda b, t: (b, t, 0)),
        out_shape=jax.ShapeDtypeStruct(x.shape, F32),
        scratch_shapes=[
            pltpu.VMEM((4, SUBLANES + tm, FFN_CHUNK), F32),
            pltpu.VMEM((SUBLANES, 2 * D_FF), F32),
            pltpu.VMEM((tm, D_MODEL), F32),
        ],
        compiler_params=pltpu.CompilerParams(
            dimension_semantics=("arbitrary", "arbitrary"), vmem_limit_bytes=VMEM_LIMIT_BYTES),
        name=f"ffn_{layer}",
    )(x, p["g_ffn_pre"], p["w_ffn_up"], p["w_ffn_conv"], p["b_ffn_conv"], p["w_ffn_down"],
      p["g_ffn_post"])


def _block_diag_gates(w_r, w_i, b_r, b_i):
    na = w_r.shape[0]
    per = MXU_DIM // LRU_BLOCK

    def bd(w):
        w = w.reshape(na, GATE_CHUNKS, per, LRU_BLOCK, LRU_BLOCK)
        eye = jnp.eye(per, dtype=w.dtype)
        full = w[:, :, :, :, None, :] * eye[None, None, :, None, :, None]
        return full.reshape(na, GATE_CHUNKS, MXU_DIM, MXU_DIM)

    w = jnp.concatenate([bd(w_r), bd(w_i)], axis=-1).astype(BF16)
    b = jnp.concatenate([b_r.reshape(na, GATE_CHUNKS, 1, MXU_DIM),
                         b_i.reshape(na, GATE_CHUNKS, 1, MXU_DIM)], axis=-1)
    return w, b


def _swap_head_pairs(w_k):
    d = w_k.shape[0]
    return w_k.reshape(d, SWA_KV_HEADS // 2, 2, HEAD_DIM)[:, :, ::-1, :].reshape(d, KV_WIDTH)


@jax.jit
def kernel(x, mem, g_mix_pre, g_mix_post, g_ffn_pre, g_ffn_post, g_mem, w_mem_kv, w_mix_out,
           w_ffn_up, w_ffn_conv, b_ffn_conv, w_ffn_down, w_in_a, w_conv_a, b_conv_a,
           w_rg_r, b_rg_r, w_rg_i, b_rg_i, lru_lambda, w_in_b, sinks_b, g_kv, w_kv):
    row = lambda a: a.reshape(a.shape[0], 1, a.shape[-1])
    w_gate, b_gate = _block_diag_gates(w_rg_r, w_rg_i, b_rg_r, b_rg_i)
    p = {
        "g_mix_pre": row(g_mix_pre), "g_mix_post": row(g_mix_post),
        "g_ffn_pre": row(g_ffn_pre), "g_ffn_post": row(g_ffn_post),
        "w_mix_out": w_mix_out.astype(BF16),
        "w_ffn_up": w_ffn_up.astype(BF16), "w_ffn_conv": w_ffn_conv,
        "b_ffn_conv": row(b_ffn_conv), "w_ffn_down": w_ffn_down.astype(BF16),
        "w_in_a": w_in_a.astype(BF16), "w_conv_a": w_conv_a, "b_conv_a": row(b_conv_a),
        "w_gate": w_gate, "b_gate": b_gate, "lru_lambda": row(lru_lambda),
        "w_in_b": w_in_b.astype(BF16), "sinks_b": sinks_b,
    }
    p["mk_pad"], p["mv_t"] = _memkv(
        mem, row(g_mem), w_mem_kv[:, :, :MEM_WIDTH].astype(BF16),
        jnp.swapaxes(w_mem_kv[:, :, MEM_WIDTH:], 1, 2).astype(BF16))
    kpad = vt = None
    for layer in range(DEPTH):
        if layer < N_A_LAYERS:
            x = _rglru_layer(x, layer, layer, p)
        else:
            if layer == N_A_LAYERS:
                w_k = w_kv[:, :KV_WIDTH]
                wk2 = jnp.concatenate([w_k, _swap_head_pairs(w_k)], axis=-1).astype(BF16)
                kpad, vt = _shared_kv(x, g_kv.reshape(1, D_MODEL), wk2, w_kv[:, KV_WIDTH:].T.astype(BF16))
            x = _swa_layer(x, kpad, vt, layer, layer - N_A_LAYERS, p)
        x = _ffn_layer(x, layer, p)
    return x
```

```python
import math

import jax
import jax.numpy as jnp
import numpy as np
from jax import lax
from jax.experimental import pallas as pl
from jax.experimental.pallas import tpu as pltpu

D_MODEL = 1024
DEPTH = 4
N_A_LAYERS = DEPTH // 2
HEAD_DIM = 64
MEM_LEN = 256
MEM_HEADS = 4
MEM_WIDTH = MEM_HEADS * HEAD_DIM
MIX_WIDTH = D_MODEL - MEM_WIDTH
LRU_WIDTH = MIX_WIDTH
LRU_BLOCK = 64
LRU_CONV = 4
LRU_C = 8.0
SWA_HEADS = MIX_WIDTH // HEAD_DIM
SWA_KV_HEADS = 4
SWA_GROUP = SWA_HEADS // SWA_KV_HEADS
KV_WIDTH = SWA_KV_HEADS * HEAD_DIM
WINDOW = 128
D_FF = 2816
FFN_CONV = 3
EPS = 1e-6

SUBLANES = 8
LANES = 128
MXU_DIM = 256
VMEM_LIMIT_BYTES = 56 * 1024 * 1024

HEADS_PER_TILE = LANES // HEAD_DIM
KPAD_WIDTH = 2 * SWA_KV_HEADS * LANES
MKPAD_WIDTH = MEM_HEADS * LANES

GATE_CHUNKS = LRU_WIDTH // MXU_DIM
FFN_CHUNK = 256
FFN_CHUNKS = D_FF // FFN_CHUNK

TM_MIX = 512
TM_FFN = 512
TM_KV = 512

BF16 = jnp.bfloat16
F32 = jnp.float32
QK_SCALE = HEAD_DIM ** -0.5


def _alibi_slopes(n):
    def pow2_slopes(m):
        start = 2.0 ** (-8.0 / m)
        return [start ** (i + 1) for i in range(m)]
    c = 2 ** int(math.floor(math.log2(n)))
    s = pow2_slopes(c)
    if c != n:
        s = s + pow2_slopes(2 * c)[0::2][: n - c]
    return [float(np.float32(v)) for v in s]


SLOPES = _alibi_slopes(SWA_HEADS)


def _rmsnorm(x, g):
    ms = jnp.mean(x * x, axis=-1, keepdims=True)
    return x * lax.rsqrt(ms + EPS) * g


def _gelu_tanh(x):
    c = math.sqrt(2.0 / math.pi)
    return 0.5 * x * (1.0 + jnp.tanh(c * (x + 0.044715 * (x * x * x))))


def _dot(a, b):
    return jnp.dot(a, b, preferred_element_type=F32)


def _dot_nt(a, b):
    return lax.dot_general(a, b, (((1,), (1,)), ((), ())), preferred_element_type=F32)


def _dot_tn(a, b):
    return lax.dot_general(a, b, (((0,), (0,)), ((), ())), preferred_element_type=F32)


def _half_tiles(k, parities):
    upper = lax.broadcasted_iota(jnp.int32, (k.shape[0], LANES), 1) >= HEAD_DIM
    tiles = []
    for i, par in enumerate(parities):
        t = k[:, i * LANES:(i + 1) * LANES]
        tiles.append(jnp.where(upper if par else ~upper, t, 0.0))
    return jnp.concatenate(tiles, axis=-1).astype(BF16)


def _softmax_t(s_t, extra=None):
    m = jnp.max(s_t, axis=0, keepdims=True)
    if extra is not None:
        m = jnp.maximum(m, extra)
    p = jnp.exp(s_t - m)
    d = jnp.sum(p, axis=0, keepdims=True)
    if extra is not None:
        d = d + jnp.exp(extra - m)
    return (p * (1.0 / d)).astype(BF16)


def _causal_dwconv(u, ubuf, w_ref, b_ref, col0):
    tm = u.shape[0]
    taps = w_ref.shape[0]
    pad = SUBLANES
    outs = []
    for i in range(u.shape[1] // LANES):
        cols = slice(col0 + i * LANES, col0 + (i + 1) * LANES)
        slab = col0 // LANES + i
        ui = u[:, i * LANES:(i + 1) * LANES]
        ubuf[slab, pad:pad + tm, :] = ui
        out = ui * w_ref[taps - 1:taps, cols] + b_ref[:, cols]
        for k in range(taps - 1):
            shift = taps - 1 - k
            out = out + ubuf[slab, pad - shift:pad - shift + tm, :] * w_ref[k:k + 1, cols]
        ubuf[slab, 0:pad, :] = ui[tm - pad:, :]
        outs.append(out)
    return jnp.concatenate(outs, axis=-1)


def _memory_attention_t(q16, tile0, mk_ref, mvt_ref, yt_ref, row0):
    s_ts = []
    for hh in range(MEM_HEADS):
        tile = tile0 + hh // HEADS_PER_TILE
        s_ts.append(_dot_nt(mk_ref[:, hh * LANES:(hh + 1) * LANES],
                            q16[:, tile * LANES:(tile + 1) * LANES]))
    p_ts = [_softmax_t(s_t) for s_t in s_ts]
    for hh in range(MEM_HEADS):
        o_t = _dot(mvt_ref[hh * HEAD_DIM:(hh + 1) * HEAD_DIM, :], p_ts[hh])
        yt_ref[row0 + hh * HEAD_DIM:row0 + (hh + 1) * HEAD_DIM, :] = o_t.astype(BF16)


def _memkv_kernel(mem_ref, g_ref, wk_ref, wvt_ref, mk_ref, mvt_ref):
    m = mem_ref[...]
    ms = jnp.mean(m * m, axis=-1, keepdims=True)
    mn = m * lax.rsqrt(ms + EPS)
    for l in range(DEPTH):
        mnl = (mn * g_ref[l]).astype(BF16)
        k = _dot(mnl, wk_ref[l])
        k4 = jnp.concatenate([k[:, (hh // 2) * LANES:(hh // 2 + 1) * LANES]
                              for hh in range(MEM_HEADS)], axis=-1)
        mk_ref[l] = _half_tiles(k4, [hh % 2 for hh in range(MEM_HEADS)])
        mvt_ref[l] = _dot_nt(wvt_ref[l], mnl).astype(BF16)


def _memkv(mem, g_mem, wk_b, wvt_b):
    bsz = mem.shape[0]
    return pl.pallas_call(
        _memkv_kernel,
        grid=(bsz,),
        in_specs=[
            pl.BlockSpec((None, MEM_LEN, D_MODEL), lambda b: (b, 0, 0)),
            pl.BlockSpec((DEPTH, 1, D_MODEL), lambda b: (0, 0, 0)),
            pl.BlockSpec((DEPTH, D_MODEL, MEM_WIDTH), lambda b: (0, 0, 0)),
            pl.BlockSpec((DEPTH, MEM_WIDTH, D_MODEL), lambda b: (0, 0, 0)),
        ],
        out_specs=[
            pl.BlockSpec((DEPTH, None, MEM_LEN, MKPAD_WIDTH), lambda b: (0, b, 0, 0)),
            pl.BlockSpec((DEPTH, None, MEM_WIDTH, MEM_LEN), lambda b: (0, b, 0, 0)),
        ],
        out_shape=[
            jax.ShapeDtypeStruct((DEPTH, bsz, MEM_LEN, MKPAD_WIDTH), BF16),
            jax.ShapeDtypeStruct((DEPTH, bsz, MEM_WIDTH, MEM_LEN), BF16),
        ],
        compiler_params=pltpu.CompilerParams(dimension_semantics=("arbitrary",)),
        name="memkv",
    )(mem, g_mem, wk_b, wvt_b)


def _kv_kernel(x_ref, g_ref, wk2_ref, wvt_ref, kpad_ref, vt_ref):
    h = _rmsnorm(x_ref[...], g_ref[...]).astype(BF16)
    k2 = _dot(h, wk2_ref[...])
    nat = [k2[:, i * LANES:(i + 1) * LANES] for i in range(2)]
    swp = [k2[:, KV_WIDTH + i * LANES:KV_WIDTH + (i + 1) * LANES] for i in range(2)]
    lo = jnp.concatenate([nat[0], swp[0], nat[1], swp[1]], axis=-1)
    hi = jnp.concatenate([swp[0], nat[0], swp[1], nat[1]], axis=-1)
    kpad_ref[:, :SWA_KV_HEADS * LANES] = _half_tiles(lo, [0] * SWA_KV_HEADS)
    kpad_ref[:, SWA_KV_HEADS * LANES:] = _half_tiles(hi, [1] * SWA_KV_HEADS)
    vt_ref[...] = _dot_nt(wvt_ref[...], h).astype(BF16)


def _shared_kv(x, g_kv, wk2_b, wvt_b):
    bsz, seq, _ = x.shape
    return pl.pallas_call(
        _kv_kernel,
        grid=(bsz, seq // TM_KV),
        in_specs=[
            pl.BlockSpec((None, TM_KV, D_MODEL), lambda b, t: (b, t, 0)),
            pl.BlockSpec((1, D_MODEL), lambda b, t: (0, 0)),
            pl.BlockSpec((D_MODEL, 2 * KV_WIDTH), lambda b, t: (0, 0)),
            pl.BlockSpec((KV_WIDTH, D_MODEL), lambda b, t: (0, 0)),
        ],
        out_specs=[
            pl.BlockSpec((None, TM_KV, KPAD_WIDTH), lambda b, t: (b, t, 0)),
            pl.BlockSpec((None, KV_WIDTH, TM_KV), lambda b, t: (b, 0, t)),
        ],
        out_shape=[
            jax.ShapeDtypeStruct((bsz, seq, KPAD_WIDTH), BF16),
            jax.ShapeDtypeStruct((bsz, KV_WIDTH, seq), BF16),
        ],
        compiler_params=pltpu.CompilerParams(dimension_semantics=("arbitrary", "arbitrary")),
        name="shared_kv",
    )(x, g_kv, wk2_b, wvt_b)


def _rglru_kernel(x_ref, gpre_ref, win_ref, wconv_ref, bconv_ref, wgate_ref, bgate_ref,
                  lam_ref, mk_ref, mvt_ref, wout_ref, gpost_ref, o_ref,
                  ubuf, a_buf, b_buf, h_buf, hcarry, ymt_buf):
    tm = x_ref.shape[0]
    pad = SUBLANES

    @pl.when(pl.program_id(1) == 0)
    def _():
        ubuf[:, 0:pad, :] = jnp.zeros((ubuf.shape[0], pad, LANES), F32)
        hcarry[...] = jnp.zeros_like(hcarry)

    x = x_ref[...]
    h = _rmsnorm(x, gpre_ref[...]).astype(BF16)
    proj = _dot(h, win_ref[...])
    u_gate = proj[:, :LRU_WIDTH]
    u_x = proj[:, LRU_WIDTH:2 * LRU_WIDTH]
    q16 = (proj[:, 2 * LRU_WIDTH:] * QK_SCALE).astype(BF16)

    _memory_attention_t(q16, 0, mk_ref, mvt_ref, ymt_buf, 0)

    xc = _causal_dwconv(u_x, ubuf, wconv_ref, bconv_ref, 0)

    xcb = xc.astype(BF16)
    rs, is_ = [], []
    for c in range(GATE_CHUNKS):
        g = _dot(xcb[:, c * MXU_DIM:(c + 1) * MXU_DIM], wgate_ref[c]) + bgate_ref[c]
        rs.append(jax.nn.sigmoid(g[:, :MXU_DIM]))
        is_.append(jax.nn.sigmoid(g[:, MXU_DIM:]))
    r = jnp.concatenate(rs, axis=-1)
    i = jnp.concatenate(is_, axis=-1)

    lam = lam_ref[...]
    softplus_neg_lam = jnp.maximum(-lam, 0.0) + jnp.log1p(jnp.exp(-jnp.abs(lam)))
    log_a = (-LRU_C) * r * softplus_neg_lam
    a = jnp.exp(log_a)
    gain = jnp.sqrt(-jnp.tanh(log_a) * (a * a + 1.0))
    a_buf[...] = a
    b_buf[...] = gain * (i * xc)

    row = lax.broadcasted_iota(jnp.int32, (SUBLANES, LRU_WIDTH), 0)

    carry = hcarry[...]
    for gi in range(tm // SUBLANES):
        rows = slice(gi * SUBLANES, (gi + 1) * SUBLANES)
        ag = a_buf[rows, :]
        bg = b_buf[rows, :]
        for k in (1, 2, 4):
            keep = row >= k
            a_sh = jnp.where(keep, pltpu.roll(ag, k, 0), 1.0)
            b_sh = jnp.where(keep, pltpu.roll(bg, k, 0), 0.0)
            bg = ag * b_sh + bg
            ag = ag * a_sh
        hg = ag * carry + bg
        h_buf[rows, :] = hg
        carry = jnp.broadcast_to(hg[SUBLANES - 1:SUBLANES, :], (SUBLANES, LRU_WIDTH))
    hcarry[...] = carry

    y_main = (h_buf[...] * _gelu_tanh(u_gate)).astype(BF16)
    y = _dot(y_main, wout_ref[:MIX_WIDTH, :]) + _dot_tn(ymt_buf[...], wout_ref[MIX_WIDTH:, :])
    o_ref[...] = x + _rmsnorm(y, gpost_ref[...])


def _rglru_layer(x, layer, j, p):
    bsz, seq, _ = x.shape
    tm = TM_MIX
    n_in = 2 * LRU_WIDTH + MEM_WIDTH
    in_specs = [
        pl.BlockSpec((None, tm, D_MODEL), lambda b, t: (b, t, 0)),
        pl.BlockSpec((None, 1, D_MODEL), lambda b, t: (layer, 0, 0)),
        pl.BlockSpec((None, D_MODEL, n_in), lambda b, t: (j, 0, 0)),
        pl.BlockSpec((None, LRU_CONV, LRU_WIDTH), lambda b, t: (j, 0, 0)),
        pl.BlockSpec((None, 1, LRU_WIDTH), lambda b, t: (j, 0, 0)),
        pl.BlockSpec((None, GATE_CHUNKS, MXU_DIM, 2 * MXU_DIM), lambda b, t: (j, 0, 0, 0)),
        pl.BlockSpec((None, GATE_CHUNKS, 1, 2 * MXU_DIM), lambda b, t: (j, 0, 0, 0)),
        pl.BlockSpec((None, 1, LRU_WIDTH), lambda b, t: (j, 0, 0)),
        pl.BlockSpec((None, None, MEM_LEN, MKPAD_WIDTH), lambda b, t: (layer, b, 0, 0)),
        pl.BlockSpec((None, None, MEM_WIDTH, MEM_LEN), lambda b, t: (layer, b, 0, 0)),
        pl.BlockSpec((None, D_MODEL, D_MODEL), lambda b, t: (layer, 0, 0)),
        pl.BlockSpec((None, 1, D_MODEL), lambda b, t: (layer, 0, 0)),
    ]
    return pl.pallas_call(
        _rglru_kernel,
        grid=(bsz, seq // tm),
        in_specs=in_specs,
        out_specs=pl.BlockSpec((None, tm, D_MODEL), lambda b, t: (b, t, 0)),
        out_shape=jax.ShapeDtypeStruct(x.shape, F32),
        scratch_shapes=[
            pltpu.VMEM((LRU_WIDTH // LANES, SUBLANES + tm, LANES), F32),
            pltpu.VMEM((tm, LRU_WIDTH), F32),
            pltpu.VMEM((tm, LRU_WIDTH), F32),
            pltpu.VMEM((tm, LRU_WIDTH), F32),
            pltpu.VMEM((SUBLANES, LRU_WIDTH), F32),
            pltpu.VMEM((MEM_WIDTH, tm), BF16),
        ],
        compiler_params=pltpu.CompilerParams(
            dimension_semantics=("arbitrary", "arbitrary"), vmem_limit_bytes=VMEM_LIMIT_BYTES),
        name=f"rglru_mixer_{layer}",
    )(x, p["g_mix_pre"], p["w_in_a"], p["w_conv_a"], p["b_conv_a"], p["w_gate"], p["b_gate"],
      p["lru_lambda"], p["mk_pad"], p["mv_t"], p["w_mix_out"], p["g_mix_post"])


def _swa_kernel(sinks_ref, x_ref, gpre_ref, win_ref, kp_ref, kc_ref, vtp_ref, vtc_ref,
                mk_ref, mvt_ref, wout_ref, gpost_ref, o_ref, k_buf, vt_buf, bias_buf, yt_buf):
    tm = x_ref.shape[0]
    nblk = tm // WINDOW
    band_keys = 2 * WINDOW

    @pl.when((pl.program_id(0) == 0) & (pl.program_id(1) == 0))
    def _():
        kj = lax.broadcasted_iota(jnp.int32, (band_keys, WINDOW), 0)
        qi = lax.broadcasted_iota(jnp.int32, (band_keys, WINDOW), 1)
        dist = qi + WINDOW - kj
        in_window = (dist >= 0) & (dist < WINDOW)
        dist_f = dist.astype(F32)
        for hd in range(SWA_HEADS):
            bias_buf[hd] = jnp.where(in_window, -SLOPES[hd] * dist_f, -jnp.inf)

    x = x_ref[...]
    h = _rmsnorm(x, gpre_ref[...]).astype(BF16)
    q16 = (_dot(h, win_ref[...]) * QK_SCALE).astype(BF16)

    k_buf[0:WINDOW, :] = kp_ref[...]
    k_buf[WINDOW:, :] = kc_ref[...]
    vt_buf[:, 0:WINDOW] = vtp_ref[...]
    vt_buf[:, WINDOW:] = vtc_ref[...]

    kj = lax.broadcasted_iota(jnp.int32, (band_keys, WINDOW), 0)
    no_prev = jnp.where(kj < jnp.where(pl.program_id(1) == 0, WINDOW, 0), -jnp.inf, 0.0)

    units = [(hd, qb) for hd in range(SWA_HEADS) for qb in range(nblk)]
    s_ts = []
    for hd, qb in units:
        g = hd // SWA_GROUP
        ktile = (hd % HEADS_PER_TILE) * SWA_KV_HEADS + g
        qtile = hd // HEADS_PER_TILE
        s_ts.append(_dot_nt(
            k_buf[qb * WINDOW:qb * WINDOW + band_keys, ktile * LANES:(ktile + 1) * LANES],
            q16[qb * WINDOW:(qb + 1) * WINDOW, qtile * LANES:(qtile + 1) * LANES]))

    _memory_attention_t(q16, MIX_WIDTH // LANES, mk_ref, mvt_ref, yt_buf, MIX_WIDTH)

    p_ts = []
    for (hd, qb), s_t in zip(units, s_ts):
        s_t = s_t + bias_buf[hd]
        if qb == 0:
            s_t = s_t + no_prev
        p_ts.append(_softmax_t(s_t, jnp.full((1, WINDOW), sinks_ref[hd], F32)))

    for (hd, qb), p_t in zip(units, p_ts):
        g = hd // SWA_GROUP
        o_t = _dot(vt_buf[g * HEAD_DIM:(g + 1) * HEAD_DIM, qb * WINDOW:qb * WINDOW + band_keys], p_t)
        yt_buf[hd * HEAD_DIM:(hd + 1) * HEAD_DIM, qb * WINDOW:(qb + 1) * WINDOW] = o_t.astype(BF16)

    y = _dot_tn(yt_buf[...], wout_ref[...])
    o_ref[...] = x + _rmsnorm(y, gpost_ref[...])


def _swa_layer(x, kpad, vt, layer, j, p):
    bsz, seq, _ = x.shape
    tm = TM_MIX
    per = tm // WINDOW
    prev_blk = lambda t: jnp.maximum(t * per - 1, 0)
    in_specs = [
        pl.BlockSpec(memory_space=pltpu.SMEM),
        pl.BlockSpec((None, tm, D_MODEL), lambda b, t: (b, t, 0)),
        pl.BlockSpec((None, 1, D_MODEL), lambda b, t: (layer, 0, 0)),
        pl.BlockSpec((None, D_MODEL, D_MODEL), lambda b, t: (j, 0, 0)),
        pl.BlockSpec((None, WINDOW, KPAD_WIDTH), lambda b, t: (b, prev_blk(t), 0)),
        pl.BlockSpec((None, tm, KPAD_WIDTH), lambda b, t: (b, t, 0)),
        pl.BlockSpec((None, KV_WIDTH, WINDOW), lambda b, t: (b, 0, prev_blk(t))),
        pl.BlockSpec((None, KV_WIDTH, tm), lambda b, t: (b, 0, t)),
        pl.BlockSpec((None, None, MEM_LEN, MKPAD_WIDTH), lambda b, t: (layer, b, 0, 0)),
        pl.BlockSpec((None, None, MEM_WIDTH, MEM_LEN), lambda b, t: (layer, b, 0, 0)),
        pl.BlockSpec((None, D_MODEL, D_MODEL), lambda b, t: (layer, 0, 0)),
        pl.BlockSpec((None, 1, D_MODEL), lambda b, t: (layer, 0, 0)),
    ]
    return pl.pallas_call(
        _swa_kernel,
        grid=(bsz, seq // tm),
        in_specs=in_specs,
        out_specs=pl.BlockSpec((None, tm, D_MODEL), lambda b, t: (b, t, 0)),
        out_shape=jax.ShapeDtypeStruct(x.shape, F32),
        scratch_shapes=[
            pltpu.VMEM((WINDOW + tm, KPAD_WIDTH), BF16),
            pltpu.VMEM((KV_WIDTH, WINDOW + tm), BF16),
            pltpu.VMEM((SWA_HEADS, 2 * WINDOW, WINDOW), F32),
            pltpu.VMEM((D_MODEL, tm), BF16),
        ],
        compiler_params=pltpu.CompilerParams(
            dimension_semantics=("arbitrary", "arbitrary"), vmem_limit_bytes=VMEM_LIMIT_BYTES),
        name=f"swa_mixer_{layer}",
    )(p["sinks_b"][j], x, p["g_mix_pre"], p["w_in_b"], kpad, kpad, vt, vt, p["mk_pad"], p["mv_t"],
      p["w_mix_out"], p["g_mix_post"])


def _ffn_kernel(x_ref, gpre_ref, wup_ref, wconv_ref, bconv_ref, wdown_ref, gpost_ref, o_ref,
                ubuf, act_buf):
    tm = x_ref.shape[0]
    pad = SUBLANES

    @pl.when(pl.program_id(1) == 0)
    def _():
        ubuf[:, 0:pad, :] = jnp.zeros((ubuf.shape[0], pad, LANES), F32)

    x = x_ref[...]
    h = _rmsnorm(x, gpre_ref[...]).astype(BF16)

    def conv(col0):
        u = _dot(h, wup_ref[:, col0:col0 + FFN_CHUNK])
        return _causal_dwconv(u, ubuf, wconv_ref, bconv_ref, col0)

    for c in range(FFN_CHUNKS):
        gate = conv(c * FFN_CHUNK)
        val = conv(D_FF + c * FFN_CHUNK)
        act_buf[:, c * FFN_CHUNK:(c + 1) * FFN_CHUNK] = (_gelu_tanh(gate) * val).astype(BF16)

    y = _dot(act_buf[...], wdown_ref[...])
    o_ref[...] = x + _rmsnorm(y, gpost_ref[...])


def _ffn_layer(x, layer, p):
    bsz, seq, _ = x.shape
    tm = TM_FFN
    in_specs = [
        pl.BlockSpec((None, tm, D_MODEL), lambda b, t: (b, t, 0)),
        pl.BlockSpec((None, 1, D_MODEL), lambda b, t: (layer, 0, 0)),
        pl.BlockSpec((None, D_MODEL, 2 * D_FF), lambda b, t: (layer, 0, 0),
                     pipeline_mode=pl.Buffered(1)),
        pl.BlockSpec((None, FFN_CONV, 2 * D_FF), lambda b, t: (layer, 0, 0)),
        pl.BlockSpec((None, 1, 2 * D_FF), lambda b, t: (layer, 0, 0)),
        pl.BlockSpec((None, D_FF, D_MODEL), lambda b, t: (layer, 0, 0),
                     pipeline_mode=pl.Buffered(1)),
        pl.BlockSpec((None, 1, D_MODEL), lambda b, t: (layer, 0, 0)),
    ]
    return pl.pallas_call(
        _ffn_kernel,
        grid=(bsz, seq // tm),
        in_specs=in_specs,
        out_specs=pl.BlockSpec((None, tm, D_MODEL), lambda b, t: (b, t, 0)),
        out_shape=jax.ShapeDtypeStruct(x.shape, F32),
        scratch_shapes=[
            pltpu.VMEM((2 * D_FF // LANES, SUBLANES + tm, LANES), F32),
            pltpu.VMEM((tm, D_FF), BF16),
        ],
        compiler_params=pltpu.CompilerParams(
            dimension_semantics=("arbitrary", "arbitrary"), vmem_limit_bytes=VMEM_LIMIT_BYTES),
        name=f"ffn_{layer}",
    )(x, p["g_ffn_pre"], p["w_ffn_up"], p["w_ffn_conv"], p["b_ffn_conv"], p["w_ffn_down"],
      p["g_ffn_post"])


def _block_diag_gates(w_r, w_i, b_r, b_i):
    na = w_r.shape[0]
    per = MXU_DIM // LRU_BLOCK

    def bd(w):
        w = w.reshape(na, GATE_CHUNKS, per, LRU_BLOCK, LRU_BLOCK)
        eye = jnp.eye(per, dtype=w.dtype)
        full = w[:, :, :, :, None, :] * eye[None, None, :, None, :, None]
        return full.reshape(na, GATE_CHUNKS, MXU_DIM, MXU_DIM)

    w = jnp.concatenate([bd(w_r), bd(w_i)], axis=-1).astype(BF16)
    b = jnp.concatenate([b_r.reshape(na, GATE_CHUNKS, 1, MXU_DIM),
                         b_i.reshape(na, GATE_CHUNKS, 1, MXU_DIM)], axis=-1)
    return w, b


def _swap_head_pairs(w_k):
    d = w_k.shape[0]
    return w_k.reshape(d, SWA_KV_HEADS // 2, 2, HEAD_DIM)[:, :, ::-1, :].reshape(d, KV_WIDTH)


@jax.jit
def kernel(x, mem, g_mix_pre, g_mix_post, g_ffn_pre, g_ffn_post, g_mem, w_mem_kv, w_mix_out,
           w_ffn_up, w_ffn_conv, b_ffn_conv, w_ffn_down, w_in_a, w_conv_a, b_conv_a,
           w_rg_r, b_rg_r, w_rg_i, b_rg_i, lru_lambda, w_in_b, sinks_b, g_kv, w_kv):
    row = lambda a: a.reshape(a.shape[0], 1, a.shape[-1])
    w_gate, b_gate = _block_diag_gates(w_rg_r, w_rg_i, b_rg_r, b_rg_i)
    p = {
        "g_mix_pre": row(g_mix_pre), "g_mix_post": row(g_mix_post),
        "g_ffn_pre": row(g_ffn_pre), "g_ffn_post": row(g_ffn_post),
        "w_mix_out": w_mix_out.astype(BF16),
        "w_ffn_up": w_ffn_up.astype(BF16), "w_ffn_conv": w_ffn_conv,
        "b_ffn_conv": row(b_ffn_conv), "w_ffn_down": w_ffn_down.astype(BF16),
        "w_in_a": w_in_a.astype(BF16), "w_conv_a": w_conv_a, "b_conv_a": row(b_conv_a),
        "w_gate": w_gate, "b_gate": b_gate, "lru_lambda": row(lru_lambda),
        "w_in_b": w_in_b.astype(BF16), "sinks_b": sinks_b,
    }
    p["mk_pad"], p["mv_t"] = _memkv(
        mem, row(g_mem), w_mem_kv[:, :, :MEM_WIDTH].astype(BF16),
        jnp.swapaxes(w_mem_kv[:, :, MEM_WIDTH:], 1, 2).astype(BF16))
    kpad = vt = None
    for layer in range(DEPTH):
        if layer < N_A_LAYERS:
            x = _rglru_layer(x, layer, layer, p)
        else:
            if layer == N_A_LAYERS:
                w_k = w_kv[:, :KV_WIDTH]
                wk2 = jnp.concatenate([w_k, _swap_head_pairs(w_k)], axis=-1).astype(BF16)
                kpad, vt = _shared_kv(x, g_kv.reshape(1, D_MODEL), wk2, w_kv[:, KV_WIDTH:].T.astype(BF16))
            x = _swa_layer(x, kpad, vt, layer, layer - N_A_LAYERS, p)
        x = _ffn_layer(x, layer, p)
    return x
```

```python
import functools
import math

import jax
import jax.numpy as jnp
import numpy as np
from jax import lax
from jax.experimental import pallas as pl
from jax.experimental.pallas import tpu as pltpu

D_MODEL = 1024
DEPTH = 4
N_A_LAYERS = DEPTH // 2
HEAD_DIM = 64
MEM_LEN = 256
MEM_HEADS = 4
MEM_WIDTH = MEM_HEADS * HEAD_DIM
MIX_WIDTH = D_MODEL - MEM_WIDTH
LRU_WIDTH = MIX_WIDTH
LRU_BLOCK = 64
LRU_CONV = 4
LRU_C = 8.0
SWA_HEADS = MIX_WIDTH // HEAD_DIM
SWA_KV_HEADS = 4
SWA_GROUP = SWA_HEADS // SWA_KV_HEADS
KV_WIDTH = SWA_KV_HEADS * HEAD_DIM
WINDOW = 128
D_FF = 2816
FFN_CONV = 3
EPS = 1e-6

SUBLANES = 8
LANES = 128
MXU_DIM = 256
VMEM_LIMIT_BYTES = 56 * 1024 * 1024

HEADS_PER_TILE = LANES // HEAD_DIM
KPAD_WIDTH = 2 * SWA_KV_HEADS * LANES
MKPAD_WIDTH = MEM_HEADS * LANES

GATE_CHUNKS = LRU_WIDTH // MXU_DIM
FFN_CHUNK = 256
FFN_CHUNKS = D_FF // FFN_CHUNK

TM_MIX = 512
MIX_SPLIT = 4
SWA_SPLIT = 2
SEG_GAP = 4
TM_FFN = 512

BF16 = jnp.bfloat16
F32 = jnp.float32
QK_SCALE = HEAD_DIM ** -0.5


def _alibi_slopes(n):
    def pow2_slopes(m):
        start = 2.0 ** (-8.0 / m)
        return [start ** (i + 1) for i in range(m)]
    c = 2 ** int(math.floor(math.log2(n)))
    s = pow2_slopes(c)
    if c != n:
        s = s + pow2_slopes(2 * c)[0::2][: n - c]
    return [float(np.float32(v)) for v in s]


SLOPES = _alibi_slopes(SWA_HEADS)


def _rmsnorm(x, g):
    ms = jnp.mean(x * x, axis=-1, keepdims=True)
    return x * lax.rsqrt(ms + EPS) * g


def _gelu_tanh(x):
    c = math.sqrt(2.0 / math.pi)
    return 0.5 * x * (1.0 + jnp.tanh(c * (x + 0.044715 * (x * x * x))))


def _dot(a, b):
    return jnp.dot(a, b, preferred_element_type=F32)


def _dot_nt(a, b):
    return lax.dot_general(a, b, (((1,), (1,)), ((), ())), preferred_element_type=F32)


def _dot_tn(a, b):
    return lax.dot_general(a, b, (((0,), (0,)), ((), ())), preferred_element_type=F32)


def _half_tiles(k, parities):
    upper = lax.broadcasted_iota(jnp.int32, (k.shape[0], LANES), 1) >= HEAD_DIM
    tiles = []
    for i, par in enumerate(parities):
        t = k[:, i * LANES:(i + 1) * LANES]
        tiles.append(jnp.where(upper if par else ~upper, t, 0.0))
    return jnp.concatenate(tiles, axis=-1).astype(BF16)


def _softmax_t(s_t, extra=None):
    m = jnp.max(s_t, axis=0, keepdims=True)
    if extra is not None:
        m = jnp.maximum(m, extra)
    p = jnp.exp(s_t - m)
    d = jnp.sum(p, axis=0, keepdims=True)
    if extra is not None:
        d = d + jnp.exp(extra - m)
    return (p * (1.0 / d)).astype(BF16)


def _causal_dwconv(u, ubuf, w_ref, b_ref, col0, row0=0, last=True):
    m = u.shape[0]
    taps = w_ref.shape[0]
    base = SUBLANES + row0
    outs = []
    for i in range(u.shape[1] // LANES):
        cols = slice(col0 + i * LANES, col0 + (i + 1) * LANES)
        slab = col0 // LANES + i
        ui = u[:, i * LANES:(i + 1) * LANES]
        ubuf[slab, base:base + m, :] = ui
        out = ui * w_ref[taps - 1:taps, cols] + b_ref[:, cols]
        for k in range(taps - 1):
            shift = taps - 1 - k
            out = out + ubuf[slab, base - shift:base - shift + m, :] * w_ref[k:k + 1, cols]
        if last:
            ubuf[slab, 0:SUBLANES, :] = ui[m - SUBLANES:, :]
        outs.append(out)
    return jnp.concatenate(outs, axis=-1)


def _memory_scores_t(q16, tile0, mk_ref):
    s_ts = []
    for hh in range(MEM_HEADS):
        tile = tile0 + hh // HEADS_PER_TILE
        s_ts.append(_dot_nt(mk_ref[:, hh * LANES:(hh + 1) * LANES],
                            q16[:, tile * LANES:(tile + 1) * LANES]))
    return s_ts


def _memory_values_t(p_ts, mvt_ref, yt_ref, row0, cols):
    for hh in range(MEM_HEADS):
        o_t = _dot(mvt_ref[hh * HEAD_DIM:(hh + 1) * HEAD_DIM, :], p_ts[hh])
        yt_ref[row0 + hh * HEAD_DIM:row0 + (hh + 1) * HEAD_DIM, cols] = o_t.astype(BF16)


def _interleave(parts):
    live = list(parts)
    while live:
        nxt = []
        for part in live:
            try:
                next(part)
                nxt.append(part)
            except StopIteration:
                pass
        live = nxt


def _memkv_kernel(mem_ref, g_ref, wk_ref, wvt_ref, mk_ref, mvt_ref):
    m = mem_ref[...]
    ms = jnp.mean(m * m, axis=-1, keepdims=True)
    mn = m * lax.rsqrt(ms + EPS)
    for l in range(DEPTH):
        mnl = (mn * g_ref[l]).astype(BF16)
        k = _dot(mnl, wk_ref[l])
        k4 = jnp.concatenate([k[:, (hh // 2) * LANES:(hh // 2 + 1) * LANES]
                              for hh in range(MEM_HEADS)], axis=-1)
        mk_ref[l] = _half_tiles(k4, [hh % 2 for hh in range(MEM_HEADS)])
        mvt_ref[l] = _dot_nt(wvt_ref[l], mnl).astype(BF16)


def _memkv(mem, g_mem, wk_b, wvt_b):
    bsz = mem.shape[0]
    return pl.pallas_call(
        _memkv_kernel,
        grid=(bsz,),
        in_specs=[
            pl.BlockSpec((None, MEM_LEN, D_MODEL), lambda b: (b, 0, 0)),
            pl.BlockSpec((DEPTH, 1, D_MODEL), lambda b: (0, 0, 0)),
            pl.BlockSpec((DEPTH, D_MODEL, MEM_WIDTH), lambda b: (0, 0, 0)),
            pl.BlockSpec((DEPTH, MEM_WIDTH, D_MODEL), lambda b: (0, 0, 0)),
        ],
        out_specs=[
            pl.BlockSpec((DEPTH, None, MEM_LEN, MKPAD_WIDTH), lambda b: (0, b, 0, 0)),
            pl.BlockSpec((DEPTH, None, MEM_WIDTH, MEM_LEN), lambda b: (0, b, 0, 0)),
        ],
        out_shape=[
            jax.ShapeDtypeStruct((DEPTH, bsz, MEM_LEN, MKPAD_WIDTH), BF16),
            jax.ShapeDtypeStruct((DEPTH, bsz, MEM_WIDTH, MEM_LEN), BF16),
        ],
        compiler_params=pltpu.CompilerParams(dimension_semantics=("arbitrary",)),
        name="memkv",
    )(mem, g_mem, wk_b, wvt_b)


def _shared_kv(x, g_ref, wk2_ref, wvt_ref, kpad_ref, vt_ref):
    h = _rmsnorm(x, g_ref[...]).astype(BF16)
    k2 = _dot(h, wk2_ref[...])
    nat = [k2[:, i * LANES:(i + 1) * LANES] for i in range(2)]
    swp = [k2[:, KV_WIDTH + i * LANES:KV_WIDTH + (i + 1) * LANES] for i in range(2)]
    lo = jnp.concatenate([nat[0], swp[0], nat[1], swp[1]], axis=-1)
    hi = jnp.concatenate([swp[0], nat[0], swp[1], nat[1]], axis=-1)
    kpad_ref[:, :SWA_KV_HEADS * LANES] = _half_tiles(lo, [0] * SWA_KV_HEADS)
    kpad_ref[:, SWA_KV_HEADS * LANES:] = _half_tiles(hi, [1] * SWA_KV_HEADS)
    vt_ref[...] = _dot_nt(wvt_ref[...], h).astype(BF16)


def _rglru_kernel(x_ref, gpre_ref, win_ref, wconv_ref, bconv_ref, wgate_ref, bgate_ref,
                  lam_ref, mk_ref, mvt_ref, wout_ref, gpost_ref, o_ref,
                  ubuf, a_seg, b_seg, h_seg, hcarry, ymt_buf):
    tm = x_ref.shape[0]
    pad = SUBLANES
    n_slab = LRU_WIDTH // LANES
    seg_len = tm // MIX_SPLIT // SUBLANES
    pitch = seg_len + SEG_GAP

    @pl.when(pl.program_id(1) == 0)
    def _():
        ubuf[:, 0:pad, :] = jnp.zeros((ubuf.shape[0], pad, LANES), F32)
        hcarry[...] = jnp.zeros_like(hcarry)

    lam = lam_ref[...]
    softplus_neg_lam = jnp.maximum(-lam, 0.0) + jnp.log1p(jnp.exp(-jnp.abs(lam)))
    row = lax.broadcasted_iota(jnp.int32, (SUBLANES, LRU_WIDTH), 0)
    state = {"carry": hcarry[...]}
    sub = tm // MIX_SPLIT

    def part(n):
        r0 = n * sub
        rows = slice(r0, r0 + sub)
        x = x_ref[rows, :]
        h = _rmsnorm(x, gpre_ref[...]).astype(BF16)
        proj = _dot(h, win_ref[...])
        u_gate = proj[:, :LRU_WIDTH]
        u_x = proj[:, LRU_WIDTH:2 * LRU_WIDTH]
        q16 = (proj[:, 2 * LRU_WIDTH:] * QK_SCALE).astype(BF16)
        s_ts = _memory_scores_t(q16, 0, mk_ref)
        yield

        xc = _causal_dwconv(u_x, ubuf, wconv_ref, bconv_ref, 0, r0, n == MIX_SPLIT - 1)
        xcb = xc.astype(BF16)
        gates = [_dot(xcb[:, c * MXU_DIM:(c + 1) * MXU_DIM], wgate_ref[c]) + bgate_ref[c]
                 for c in range(GATE_CHUNKS)]
        _memory_values_t([_softmax_t(s_t) for s_t in s_ts], mvt_ref, ymt_buf, 0, rows)
        yield

        r = jnp.concatenate([jax.nn.sigmoid(g[:, :MXU_DIM]) for g in gates], axis=-1)
        i = jnp.concatenate([jax.nn.sigmoid(g[:, MXU_DIM:]) for g in gates], axis=-1)
        log_a = (-LRU_C) * r * softplus_neg_lam
        a = jnp.exp(log_a)
        gain = jnp.sqrt(-jnp.tanh(log_a) * (a * a + 1.0))
        b = gain * (i * xc)

        for sl in range(n_slab):
            lanes = slice(sl * LANES, (sl + 1) * LANES)
            for s in range(SUBLANES):
                seg = slice(s * pitch, s * pitch + seg_len)
                src = slice(s * seg_len, (s + 1) * seg_len)
                a_seg[n * n_slab + sl, seg, :] = a[src, lanes]
                b_seg[n * n_slab + sl, seg, :] = b[src, lanes]

        def seg_row(ref, j):
            return jnp.concatenate(
                [ref[n * n_slab + sl, pl.ds(j, SUBLANES, stride=pitch), :] for sl in range(n_slab)], axis=-1)

        a_end = seg_row(a_seg, 0)
        h_end = seg_row(b_seg, 0)
        for j in range(1, seg_len):
            aj = seg_row(a_seg, j)
            h_end = aj * h_end + seg_row(b_seg, j)
            a_end = aj * a_end
        for k in (1, 2, 4):
            keep = row >= k
            a_sh = jnp.where(keep, pltpu.roll(a_end, k, 0), 1.0)
            h_sh = jnp.where(keep, pltpu.roll(h_end, k, 0), 0.0)
            h_end = a_end * h_sh + h_end
            a_end = a_end * a_sh
        ends = h_end + a_end * state["carry"]
        hj = jnp.where(row >= 1, pltpu.roll(ends, 1, 0), state["carry"])
        state["carry"] = jnp.broadcast_to(ends[SUBLANES - 1:SUBLANES, :], (SUBLANES, LRU_WIDTH))
        for j in range(seg_len):
            hj = seg_row(a_seg, j) * hj + seg_row(b_seg, j)
            for sl in range(n_slab):
                h_seg[n * n_slab + sl, pl.ds(j, SUBLANES, stride=pitch), :] = hj[:, sl * LANES:(sl + 1) * LANES]
        yield

        h_nat = jnp.concatenate(
            [jnp.concatenate([h_seg[n * n_slab + sl, s * pitch:s * pitch + seg_len, :]
                              for s in range(SUBLANES)], axis=0) for sl in range(n_slab)], axis=-1)
        y_main = (h_nat * _gelu_tanh(u_gate)).astype(BF16)
        y = (_dot(y_main, wout_ref[:MIX_WIDTH, :])
             + _dot_tn(ymt_buf[:, rows], wout_ref[MIX_WIDTH:, :]))
        o_ref[rows, :] = x + _rmsnorm(y, gpost_ref[...])

    _interleave([part(n) for n in range(MIX_SPLIT)])
    hcarry[...] = state["carry"]


def _rglru_layer(x, layer, j, p):
    bsz, seq, _ = x.shape
    tm = TM_MIX
    n_in = 2 * LRU_WIDTH + MEM_WIDTH
    in_specs = [
        pl.BlockSpec((None, tm, D_MODEL), lambda b, t: (b, t, 0)),
        pl.BlockSpec((None, 1, D_MODEL), lambda b, t: (layer, 0, 0)),
        pl.BlockSpec((None, D_MODEL, n_in), lambda b, t: (j, 0, 0)),
        pl.BlockSpec((None, LRU_CONV, LRU_WIDTH), lambda b, t: (j, 0, 0)),
        pl.BlockSpec((None, 1, LRU_WIDTH), lambda b, t: (j, 0, 0)),
        pl.BlockSpec((None, GATE_CHUNKS, MXU_DIM, 2 * MXU_DIM), lambda b, t: (j, 0, 0, 0)),
        pl.BlockSpec((None, GATE_CHUNKS, 1, 2 * MXU_DIM), lambda b, t: (j, 0, 0, 0)),
        pl.BlockSpec((None, 1, LRU_WIDTH), lambda b, t: (j, 0, 0)),
        pl.BlockSpec((None, None, MEM_LEN, MKPAD_WIDTH), lambda b, t: (layer, b, 0, 0)),
        pl.BlockSpec((None, None, MEM_WIDTH, MEM_LEN), lambda b, t: (layer, b, 0, 0)),
        pl.BlockSpec((None, D_MODEL, D_MODEL), lambda b, t: (layer, 0, 0)),
        pl.BlockSpec((None, 1, D_MODEL), lambda b, t: (layer, 0, 0)),
    ]
    return pl.pallas_call(
        _rglru_kernel,
        grid=(bsz, seq // tm),
        in_specs=in_specs,
        out_specs=pl.BlockSpec((None, tm, D_MODEL), lambda b, t: (b, t, 0)),
        out_shape=jax.ShapeDtypeStruct(x.shape, F32),
        scratch_shapes=[
            pltpu.VMEM((LRU_WIDTH // LANES, SUBLANES + tm, LANES), F32),
            pltpu.VMEM((MIX_SPLIT * LRU_WIDTH // LANES, tm // MIX_SPLIT + SUBLANES * SEG_GAP, LANES), F32),
            pltpu.VMEM((MIX_SPLIT * LRU_WIDTH // LANES, tm // MIX_SPLIT + SUBLANES * SEG_GAP, LANES), F32),
            pltpu.VMEM((MIX_SPLIT * LRU_WIDTH // LANES, tm // MIX_SPLIT + SUBLANES * SEG_GAP, LANES), F32),
            pltpu.VMEM((SUBLANES, LRU_WIDTH), F32),
            pltpu.VMEM((MEM_WIDTH, tm), BF16),
        ],
        compiler_params=pltpu.CompilerParams(
            dimension_semantics=("arbitrary", "arbitrary"), vmem_limit_bytes=VMEM_LIMIT_BYTES),
        name=f"rglru_mixer_{layer}",
    )(x, p["g_mix_pre"], p["w_in_a"], p["w_conv_a"], p["b_conv_a"], p["w_gate"], p["b_gate"],
      p["lru_lambda"], p["mk_pad"], p["mv_t"], p["w_mix_out"], p["g_mix_post"])


def _swa_kernel(sinks_ref, x_ref, gpre_ref, win_ref, kp_ref, kc_ref, vtp_ref, vtc_ref,
                mk_ref, mvt_ref, wout_ref, gpost_ref, o_ref, k_buf, vt_buf, bias_buf, yt_buf):
    tm = x_ref.shape[0]
    band_keys = 2 * WINDOW

    @pl.when((pl.program_id(0) == 0) & (pl.program_id(1) == 0))
    def _():
        kj = lax.broadcasted_iota(jnp.int32, (band_keys, WINDOW), 0)
        qi = lax.broadcasted_iota(jnp.int32, (band_keys, WINDOW), 1)
        dist = qi + WINDOW - kj
        in_window = (dist >= 0) & (dist < WINDOW)
        dist_f = dist.astype(F32)
        for hd in range(SWA_HEADS):
            bias_buf[hd] = jnp.where(in_window, -SLOPES[hd] * dist_f, -jnp.inf)

    k_buf[0:WINDOW, :] = kp_ref[...]
    k_buf[WINDOW:, :] = kc_ref[...]
    vt_buf[:, 0:WINDOW] = vtp_ref[...]
    vt_buf[:, WINDOW:] = vtc_ref[...]

    kj = lax.broadcasted_iota(jnp.int32, (band_keys, WINDOW), 0)
    no_prev = jnp.where(kj < jnp.where(pl.program_id(1) == 0, WINDOW, 0), -jnp.inf, 0.0)
    sub = tm // SWA_SPLIT

    def part(n):
        r0 = n * sub
        rows = slice(r0, r0 + sub)
        x = x_ref[rows, :]
        h = _rmsnorm(x, gpre_ref[...]).astype(BF16)
        q16 = (_dot(h, win_ref[...]) * QK_SCALE).astype(BF16)
        yield

        units = [(hd, qb) for hd in range(SWA_HEADS) for qb in range(r0 // WINDOW, (r0 + sub) // WINDOW)]
        s_ts = []
        for hd, qb in units:
            g = hd // SWA_GROUP
            ktile = (hd % HEADS_PER_TILE) * SWA_KV_HEADS + g
            qtile = hd // HEADS_PER_TILE
            s_ts.append(_dot_nt(
                k_buf[qb * WINDOW:qb * WINDOW + band_keys, ktile * LANES:(ktile + 1) * LANES],
                q16[qb * WINDOW - r0:(qb + 1) * WINDOW - r0, qtile * LANES:(qtile + 1) * LANES]))
        ms_ts = _memory_scores_t(q16, MIX_WIDTH // LANES, mk_ref)
        yield

        p_ts = []
        for (hd, qb), s_t in zip(units, s_ts):
            s_t = s_t + bias_buf[hd]
            if qb == 0:
                s_t = s_t + no_prev
            p_ts.append(_softmax_t(s_t, jnp.full((1, WINDOW), sinks_ref[hd], F32)))
        mp_ts = [_softmax_t(s_t) for s_t in ms_ts]
        yield

        for (hd, qb), p_t in zip(units, p_ts):
            g = hd // SWA_GROUP
            o_t = _dot(vt_buf[g * HEAD_DIM:(g + 1) * HEAD_DIM, qb * WINDOW:qb * WINDOW + band_keys], p_t)
            yt_buf[hd * HEAD_DIM:(hd + 1) * HEAD_DIM, qb * WINDOW:(qb + 1) * WINDOW] = o_t.astype(BF16)
        _memory_values_t(mp_ts, mvt_ref, yt_buf, MIX_WIDTH, rows)
        yield

        y = _dot_tn(yt_buf[:, rows], wout_ref[...])
        o_ref[rows, :] = x + _rmsnorm(y, gpost_ref[...])

    _interleave([part(n) for n in range(SWA_SPLIT)])


def _swa_layer(x, kpad, vt, layer, j, p):
    bsz, seq, _ = x.shape
    tm = TM_MIX
    per = tm // WINDOW
    prev_blk = lambda t: jnp.maximum(t * per - 1, 0)
    in_specs = [
        pl.BlockSpec(memory_space=pltpu.SMEM),
        pl.BlockSpec((None, tm, D_MODEL), lambda b, t: (b, t, 0)),
        pl.BlockSpec((None, 1, D_MODEL), lambda b, t: (layer, 0, 0)),
        pl.BlockSpec((None, D_MODEL, D_MODEL), lambda b, t: (j, 0, 0)),
        pl.BlockSpec((None, WINDOW, KPAD_WIDTH), lambda b, t: (b, prev_blk(t), 0)),
        pl.BlockSpec((None, tm, KPAD_WIDTH), lambda b, t: (b, t, 0)),
        pl.BlockSpec((None, KV_WIDTH, WINDOW), lambda b, t: (b, 0, prev_blk(t))),
        pl.BlockSpec((None, KV_WIDTH, tm), lambda b, t: (b, 0, t)),
        pl.BlockSpec((None, None, MEM_LEN, MKPAD_WIDTH), lambda b, t: (layer, b, 0, 0)),
        pl.BlockSpec((None, None, MEM_WIDTH, MEM_LEN), lambda b, t: (layer, b, 0, 0)),
        pl.BlockSpec((None, D_MODEL, D_MODEL), lambda b, t: (layer, 0, 0)),
        pl.BlockSpec((None, 1, D_MODEL), lambda b, t: (layer, 0, 0)),
    ]
    return pl.pallas_call(
        _swa_kernel,
        grid=(bsz, seq // tm),
        in_specs=in_specs,
        out_specs=pl.BlockSpec((None, tm, D_MODEL), lambda b, t: (b, t, 0)),
        out_shape=jax.ShapeDtypeStruct(x.shape, F32),
        scratch_shapes=[
            pltpu.VMEM((WINDOW + tm, KPAD_WIDTH), BF16),
            pltpu.VMEM((KV_WIDTH, WINDOW + tm), BF16),
            pltpu.VMEM((SWA_HEADS, 2 * WINDOW, WINDOW), F32),
            pltpu.VMEM((D_MODEL, tm), BF16),
        ],
        compiler_params=pltpu.CompilerParams(
            dimension_semantics=("arbitrary", "arbitrary"), vmem_limit_bytes=VMEM_LIMIT_BYTES),
        name=f"swa_mixer_{layer}",
    )(p["sinks_b"][j], x, p["g_mix_pre"], p["w_in_b"], kpad, kpad, vt, vt, p["mk_pad"], p["mv_t"],
      p["w_mix_out"], p["g_mix_post"])


def _ffn_kernel(x_ref, gpre_ref, wup_ref, wconv_ref, bconv_ref, wdown_ref, gpost_ref, *rest,
                emit_kv):
    if emit_kv:
        gkv_ref, wk2_ref, wvt_ref, o_ref, kpad_ref, vt_ref, ubuf, act_buf = rest
    else:
        o_ref, ubuf, act_buf = rest
    tm = x_ref.shape[0]
    pad = SUBLANES

    @pl.when(pl.program_id(1) == 0)
    def _():
        ubuf[:, 0:pad, :] = jnp.zeros((ubuf.shape[0], pad, LANES), F32)

    x = x_ref[...]
    h = _rmsnorm(x, gpre_ref[...]).astype(BF16)

    def conv(col0):
        u = _dot(h, wup_ref[:, col0:col0 + FFN_CHUNK])
        return _causal_dwconv(u, ubuf, wconv_ref, bconv_ref, col0)

    for c in range(FFN_CHUNKS):
        gate = conv(c * FFN_CHUNK)
        val = conv(D_FF + c * FFN_CHUNK)
        act_buf[:, c * FFN_CHUNK:(c + 1) * FFN_CHUNK] = (_gelu_tanh(gate) * val).astype(BF16)

    y = _dot(act_buf[...], wdown_ref[...])
    out = x + _rmsnorm(y, gpost_ref[...])
    o_ref[...] = out
    if emit_kv:
        _shared_kv(out, gkv_ref, wk2_ref, wvt_ref, kpad_ref, vt_ref)


def _ffn_layer(x, layer, p, kv_params=None):
    bsz, seq, _ = x.shape
    tm = TM_FFN
    emit_kv = kv_params is not None
    out_specs = [pl.BlockSpec((None, tm, D_MODEL), lambda b, t: (b, t, 0))]
    out_shape = [jax.ShapeDtypeStruct(x.shape, F32)]
    kv_specs = []
    if emit_kv:
        kv_specs = [
            pl.BlockSpec((1, D_MODEL), lambda b, t: (0, 0)),
            pl.BlockSpec((D_MODEL, 2 * KV_WIDTH), lambda b, t: (0, 0)),
            pl.BlockSpec((KV_WIDTH, D_MODEL), lambda b, t: (0, 0)),
        ]
        out_specs += [
            pl.BlockSpec((None, tm, KPAD_WIDTH), lambda b, t: (b, t, 0)),
            pl.BlockSpec((None, KV_WIDTH, tm), lambda b, t: (b, 0, t)),
        ]
        out_shape += [
            jax.ShapeDtypeStruct((bsz, seq, KPAD_WIDTH), BF16),
            jax.ShapeDtypeStruct((bsz, KV_WIDTH, seq), BF16),
        ]
    in_specs = [
        pl.BlockSpec((None, tm, D_MODEL), lambda b, t: (b, t, 0)),
        pl.BlockSpec((None, 1, D_MODEL), lambda b, t: (layer, 0, 0)),
        pl.BlockSpec((None, D_MODEL, 2 * D_FF), lambda b, t: (layer, 0, 0),
                     pipeline_mode=pl.Buffered(1)),
        pl.BlockSpec((None, FFN_CONV, 2 * D_FF), lambda b, t: (layer, 0, 0)),
        pl.BlockSpec((None, 1, 2 * D_FF), lambda b, t: (layer, 0, 0)),
        pl.BlockSpec((None, D_FF, D_MODEL), lambda b, t: (layer, 0, 0),
                     pipeline_mode=pl.Buffered(1)),
        pl.BlockSpec((None, 1, D_MODEL), lambda b, t: (layer, 0, 0)),
    ] + kv_specs
    outs = pl.pallas_call(
        functools.partial(_ffn_kernel, emit_kv=emit_kv),
        grid=(bsz, seq // tm),
        in_specs=in_specs,
        out_specs=out_specs,
        out_shape=out_shape,
        scratch_shapes=[
            pltpu.VMEM((2 * D_FF // LANES, SUBLANES + tm, LANES), F32),
            pltpu.VMEM((tm, D_FF), BF16),
        ],
        compiler_params=pltpu.CompilerParams(
            dimension_semantics=("arbitrary", "arbitrary"), vmem_limit_bytes=VMEM_LIMIT_BYTES),
        name=f"ffn_{layer}",
    )(x, p["g_ffn_pre"], p["w_ffn_up"], p["w_ffn_conv"], p["b_ffn_conv"], p["w_ffn_down"],
      p["g_ffn_post"], *(kv_params or ()))
    return outs if emit_kv else outs[0]


def _block_diag_gates(w_r, w_i, b_r, b_i):
    na = w_r.shape[0]
    per = MXU_DIM // LRU_BLOCK

    def bd(w):
        w = w.reshape(na, GATE_CHUNKS, per, LRU_BLOCK, LRU_BLOCK)
        eye = jnp.eye(per, dtype=w.dtype)
        full = w[:, :, :, :, None, :] * eye[None, None, :, None, :, None]
        return full.reshape(na, GATE_CHUNKS, MXU_DIM, MXU_DIM)

    w = jnp.concatenate([bd(w_r), bd(w_i)], axis=-1).astype(BF16)
    b = jnp.concatenate([b_r.reshape(na, GATE_CHUNKS, 1, MXU_DIM),
                         b_i.reshape(na, GATE_CHUNKS, 1, MXU_DIM)], axis=-1)
    return w, b


def _swap_head_pairs(w_k):
    d = w_k.shape[0]
    return w_k.reshape(d, SWA_KV_HEADS // 2, 2, HEAD_DIM)[:, :, ::-1, :].reshape(d, KV_WIDTH)


@jax.jit
def kernel(x, mem, g_mix_pre, g_mix_post, g_ffn_pre, g_ffn_post, g_mem, w_mem_kv, w_mix_out,
           w_ffn_up, w_ffn_conv, b_ffn_conv, w_ffn_down, w_in_a, w_conv_a, b_conv_a,
           w_rg_r, b_rg_r, w_rg_i, b_rg_i, lru_lambda, w_in_b, sinks_b, g_kv, w_kv):
    row = lambda a: a.reshape(a.shape[0], 1, a.shape[-1])
    w_gate, b_gate = _block_diag_gates(w_rg_r, w_rg_i, b_rg_r, b_rg_i)
    p = {
        "g_mix_pre": row(g_mix_pre), "g_mix_post": row(g_mix_post),
        "g_ffn_pre": row(g_ffn_pre), "g_ffn_post": row(g_ffn_post),
        "w_mix_out": w_mix_out.astype(BF16),
        "w_ffn_up": w_ffn_up.astype(BF16), "w_ffn_conv": w_ffn_conv,
        "b_ffn_conv": row(b_ffn_conv), "w_ffn_down": w_ffn_down.astype(BF16),
        "w_in_a": w_in_a.astype(BF16), "w_conv_a": w_conv_a, "b_conv_a": row(b_conv_a),
        "w_gate": w_gate, "b_gate": b_gate, "lru_lambda": row(lru_lambda),
        "w_in_b": w_in_b.astype(BF16), "sinks_b": sinks_b,
    }
    p["mk_pad"], p["mv_t"] = _memkv(
        mem, row(g_mem), w_mem_kv[:, :, :MEM_WIDTH].astype(BF16),
        jnp.swapaxes(w_mem_kv[:, :, MEM_WIDTH:], 1, 2).astype(BF16))
    w_k = w_kv[:, :KV_WIDTH]
    kv_params = (g_kv.reshape(1, D_MODEL),
                 jnp.concatenate([w_k, _swap_head_pairs(w_k)], axis=-1).astype(BF16),
                 w_kv[:, KV_WIDTH:].T.astype(BF16))
    kpad = vt = None
    for layer in range(DEPTH):
        if layer < N_A_LAYERS:
            x = _rglru_layer(x, layer, layer, p)
        else:
            x = _swa_layer(x, kpad, vt, layer, layer - N_A_LAYERS, p)
        if layer == N_A_LAYERS - 1:
            x, kpad, vt = _ffn_layer(x, layer, p, kv_params)
        else:
            x = _ffn_layer(x, layer, p)
    return x
```

```python
import functools
import math

import jax
import jax.numpy as jnp
import numpy as np
from jax import lax
from jax.experimental import pallas as pl
from jax.experimental.pallas import tpu as pltpu

D_MODEL = 1024
DEPTH = 4
N_A_LAYERS = DEPTH // 2
HEAD_DIM = 64
MEM_LEN = 256
MEM_HEADS = 4
MEM_WIDTH = MEM_HEADS * HEAD_DIM
MIX_WIDTH = D_MODEL - MEM_WIDTH
LRU_WIDTH = MIX_WIDTH
LRU_BLOCK = 64
LRU_CONV = 4
LRU_C = 8.0
SWA_HEADS = MIX_WIDTH // HEAD_DIM
SWA_KV_HEADS = 4
SWA_GROUP = SWA_HEADS // SWA_KV_HEADS
KV_WIDTH = SWA_KV_HEADS * HEAD_DIM
WINDOW = 128
D_FF = 2816
FFN_CONV = 3
EPS = 1e-6

SUBLANES = 8
LANES = 128
MXU_DIM = 256
VMEM_LIMIT_BYTES = 56 * 1024 * 1024

HEADS_PER_TILE = LANES // HEAD_DIM
KPAD_WIDTH = 2 * SWA_KV_HEADS * LANES
MKPAD_WIDTH = MEM_HEADS * LANES

GATE_CHUNKS = LRU_WIDTH // MXU_DIM
FFN_CHUNK = 256
FFN_CHUNKS = D_FF // FFN_CHUNK

TM_MIX = 512
MIX_SPLIT = 4
SWA_SPLIT = 1
CAST_ROWS_UP = 16
CAST_ROWS_DOWN = 64
SEG_GAP = 4
TM_FFN = 512

BF16 = jnp.bfloat16
F32 = jnp.float32
QK_SCALE = HEAD_DIM ** -0.5


def _alibi_slopes(n):
    def pow2_slopes(m):
        start = 2.0 ** (-8.0 / m)
        return [start ** (i + 1) for i in range(m)]
    c = 2 ** int(math.floor(math.log2(n)))
    s = pow2_slopes(c)
    if c != n:
        s = s + pow2_slopes(2 * c)[0::2][: n - c]
    return [float(np.float32(v)) for v in s]


SLOPES = _alibi_slopes(SWA_HEADS)


def _rmsnorm(x, g):
    ms = jnp.mean(x * x, axis=-1, keepdims=True)
    return x * lax.rsqrt(ms + EPS) * g


def _gelu_tanh(x):
    c = math.sqrt(2.0 / math.pi)
    return 0.5 * x * (1.0 + jnp.tanh(c * (x + 0.044715 * (x * x * x))))


def _dot(a, b):
    return jnp.dot(a, b, preferred_element_type=F32)


def _dot_nt(a, b):
    return lax.dot_general(a, b, (((1,), (1,)), ((), ())), preferred_element_type=F32)


def _dot_tn(a, b):
    return lax.dot_general(a, b, (((0,), (0,)), ((), ())), preferred_element_type=F32)


def _half_tiles(k, parities):
    upper = lax.broadcasted_iota(jnp.int32, (k.shape[0], LANES), 1) >= HEAD_DIM
    tiles = []
    for i, par in enumerate(parities):
        t = k[:, i * LANES:(i + 1) * LANES]
        tiles.append(jnp.where(upper if par else ~upper, t, 0.0))
    return jnp.concatenate(tiles, axis=-1).astype(BF16)


def _softmax_t(s_t, extra=None):
    m = jnp.max(s_t, axis=0, keepdims=True)
    if extra is not None:
        m = jnp.maximum(m, extra)
    p = jnp.exp(s_t - m)
    d = jnp.sum(p, axis=0, keepdims=True)
    if extra is not None:
        d = d + jnp.exp(extra - m)
    return (p * (1.0 / d)).astype(BF16)


def _causal_dwconv(u, ubuf, w_ref, b_ref, col0, row0=0, last=True):
    m = u.shape[0]
    taps = w_ref.shape[0]
    base = SUBLANES + row0
    outs = []
    for i in range(u.shape[1] // LANES):
        cols = slice(col0 + i * LANES, col0 + (i + 1) * LANES)
        slab = col0 // LANES + i
        ui = u[:, i * LANES:(i + 1) * LANES]
        ubuf[slab, base:base + m, :] = ui
        out = ui * w_ref[taps - 1:taps, cols] + b_ref[:, cols]
        for k in range(taps - 1):
            shift = taps - 1 - k
            out = out + ubuf[slab, base - shift:base - shift + m, :] * w_ref[k:k + 1, cols]
        if last:
            ubuf[slab, 0:SUBLANES, :] = ui[m - SUBLANES:, :]
        outs.append(out)
    return jnp.concatenate(outs, axis=-1)


def _memory_scores_t(q16, tile0, mk_ref):
    s_ts = []
    for hh in range(MEM_HEADS):
        tile = tile0 + hh // HEADS_PER_TILE
        s_ts.append(_dot_nt(mk_ref[:, hh * LANES:(hh + 1) * LANES],
                            q16[:, tile * LANES:(tile + 1) * LANES]))
    return s_ts


def _memory_values_t(p_ts, mvt_ref, yt_ref, row0, cols):
    for hh in range(MEM_HEADS):
        o_t = _dot(mvt_ref[hh * HEAD_DIM:(hh + 1) * HEAD_DIM, :], p_ts[hh])
        yt_ref[row0 + hh * HEAD_DIM:row0 + (hh + 1) * HEAD_DIM, cols] = o_t.astype(BF16)


def _interleave(parts):
    live = list(parts)
    while live:
        nxt = []
        for part in live:
            try:
                next(part)
                nxt.append(part)
            except StopIteration:
                pass
        live = nxt


def _ffn_weight_cast_specs(layer, bsz, nt):
    steps = bsz * nt
    assert steps * CAST_ROWS_UP == D_MODEL and steps * CAST_ROWS_DOWN >= D_FF and D_FF % CAST_ROWS_DOWN == 0
    up_blk = lambda b, t: b * nt + t
    down_blk = lambda b, t: jnp.minimum(b * nt + t, D_FF // CAST_ROWS_DOWN - 1)
    in_specs = [pl.BlockSpec((None, CAST_ROWS_UP, 2 * D_FF), lambda b, t: (layer, up_blk(b, t), 0)),
                pl.BlockSpec((None, CAST_ROWS_DOWN, D_MODEL), lambda b, t: (layer, down_blk(b, t), 0))]
    out_specs = [pl.BlockSpec((CAST_ROWS_UP, 2 * D_FF), lambda b, t: (up_blk(b, t), 0)),
                 pl.BlockSpec((CAST_ROWS_DOWN, D_MODEL), lambda b, t: (down_blk(b, t), 0))]
    out_shape = [jax.ShapeDtypeStruct((D_MODEL, 2 * D_FF), BF16),
                 jax.ShapeDtypeStruct((D_FF, D_MODEL), BF16)]
    return in_specs, out_specs, out_shape


def _memkv_kernel(mem_ref, g_ref, wk_ref, wvt_ref, mk_ref, mvt_ref):
    m = mem_ref[...]
    ms = jnp.mean(m * m, axis=-1, keepdims=True)
    mn = m * lax.rsqrt(ms + EPS)
    for l in range(DEPTH):
        mnl = (mn * g_ref[l]).astype(BF16)
        k = _dot(mnl, wk_ref[l])
        k4 = jnp.concatenate([k[:, (hh // 2) * LANES:(hh // 2 + 1) * LANES]
                              for hh in range(MEM_HEADS)], axis=-1)
        mk_ref[l] = _half_tiles(k4, [hh % 2 for hh in range(MEM_HEADS)])
        mvt_ref[l] = _dot_nt(wvt_ref[l], mnl).astype(BF16)


def _memkv(mem, g_mem, wk_b, wvt_b):
    bsz = mem.shape[0]
    return pl.pallas_call(
        _memkv_kernel,
        grid=(bsz,),
        in_specs=[
            pl.BlockSpec((None, MEM_LEN, D_MODEL), lambda b: (b, 0, 0)),
            pl.BlockSpec((DEPTH, 1, D_MODEL), lambda b: (0, 0, 0)),
            pl.BlockSpec((DEPTH, D_MODEL, MEM_WIDTH), lambda b: (0, 0, 0)),
            pl.BlockSpec((DEPTH, MEM_WIDTH, D_MODEL), lambda b: (0, 0, 0)),
        ],
        out_specs=[
            pl.BlockSpec((DEPTH, None, MEM_LEN, MKPAD_WIDTH), lambda b: (0, b, 0, 0)),
            pl.BlockSpec((DEPTH, None, MEM_WIDTH, MEM_LEN), lambda b: (0, b, 0, 0)),
        ],
        out_shape=[
            jax.ShapeDtypeStruct((DEPTH, bsz, MEM_LEN, MKPAD_WIDTH), BF16),
            jax.ShapeDtypeStruct((DEPTH, bsz, MEM_WIDTH, MEM_LEN), BF16),
        ],
        compiler_params=pltpu.CompilerParams(dimension_semantics=("arbitrary",)),
        name="memkv",
    )(mem, g_mem, wk_b, wvt_b)


def _shared_kv(x, g_ref, wk2_ref, wvt_ref, kpad_ref, vt_ref):
    h = _rmsnorm(x, g_ref[...]).astype(BF16)
    k2 = _dot(h, wk2_ref[...])
    nat = [k2[:, i * LANES:(i + 1) * LANES] for i in range(2)]
    swp = [k2[:, KV_WIDTH + i * LANES:KV_WIDTH + (i + 1) * LANES] for i in range(2)]
    lo = jnp.concatenate([nat[0], swp[0], nat[1], swp[1]], axis=-1)
    hi = jnp.concatenate([swp[0], nat[0], swp[1], nat[1]], axis=-1)
    kpad_ref[:, :SWA_KV_HEADS * LANES] = _half_tiles(lo, [0] * SWA_KV_HEADS)
    kpad_ref[:, SWA_KV_HEADS * LANES:] = _half_tiles(hi, [1] * SWA_KV_HEADS)
    vt_ref[...] = _dot_nt(wvt_ref[...], h).astype(BF16)


def _rglru_kernel(x_ref, gpre_ref, win_ref, wconv_ref, bconv_ref, wgate_ref, bgate_ref,
                  lam_ref, mk_ref, mvt_ref, wout_ref, gpost_ref, wupf_ref, wdnf_ref,
                  o_ref, wupb_ref, wdnb_ref, ubuf, a_seg, b_seg, h_seg, hcarry, ymt_buf):
    wupb_ref[...] = wupf_ref[...].astype(BF16)
    wdnb_ref[...] = wdnf_ref[...].astype(BF16)
    tm = x_ref.shape[0]
    pad = SUBLANES
    n_slab = LRU_WIDTH // LANES
    seg_len = tm // MIX_SPLIT // SUBLANES
    pitch = seg_len + SEG_GAP

    @pl.when(pl.program_id(1) == 0)
    def _():
        ubuf[:, 0:pad, :] = jnp.zeros((ubuf.shape[0], pad, LANES), F32)
        hcarry[...] = jnp.zeros_like(hcarry)

    lam = lam_ref[...]
    softplus_neg_lam = jnp.maximum(-lam, 0.0) + jnp.log1p(jnp.exp(-jnp.abs(lam)))
    row = lax.broadcasted_iota(jnp.int32, (SUBLANES, LRU_WIDTH), 0)
    state = {"carry": hcarry[...]}
    sub = tm // MIX_SPLIT

    def part(n):
        r0 = n * sub
        rows = slice(r0, r0 + sub)
        x = x_ref[rows, :]
        h = _rmsnorm(x, gpre_ref[...]).astype(BF16)
        proj = _dot(h, win_ref[...])
        u_gate = proj[:, :LRU_WIDTH]
        u_x = proj[:, LRU_WIDTH:2 * LRU_WIDTH]
        q16 = (proj[:, 2 * LRU_WIDTH:] * QK_SCALE).astype(BF16)
        s_ts = _memory_scores_t(q16, 0, mk_ref)
        yield

        xc = _causal_dwconv(u_x, ubuf, wconv_ref, bconv_ref, 0, r0, n == MIX_SPLIT - 1)
        xcb = xc.astype(BF16)
        gates = [_dot(xcb[:, c * MXU_DIM:(c + 1) * MXU_DIM], wgate_ref[c]) + bgate_ref[c]
                 for c in range(GATE_CHUNKS)]
        _memory_values_t([_softmax_t(s_t) for s_t in s_ts], mvt_ref, ymt_buf, 0, rows)
        yield

        r = jnp.concatenate([jax.nn.sigmoid(g[:, :MXU_DIM]) for g in gates], axis=-1)
        i = jnp.concatenate([jax.nn.sigmoid(g[:, MXU_DIM:]) for g in gates], axis=-1)
        log_a = (-LRU_C) * r * softplus_neg_lam
        a = jnp.exp(log_a)
        gain = jnp.sqrt(-jnp.tanh(log_a) * (a * a + 1.0))
        b = gain * (i * xc)

        for sl in range(n_slab):
            lanes = slice(sl * LANES, (sl + 1) * LANES)
            for s in range(SUBLANES):
                seg = slice(s * pitch, s * pitch + seg_len)
                src = slice(s * seg_len, (s + 1) * seg_len)
                a_seg[n * n_slab + sl, seg, :] = a[src, lanes]
                b_seg[n * n_slab + sl, seg, :] = b[src, lanes]

        def seg_row(ref, j):
            return jnp.concatenate(
                [ref[n * n_slab + sl, pl.ds(j, SUBLANES, stride=pitch), :] for sl in range(n_slab)], axis=-1)

        a_end = seg_row(a_seg, 0)
        h_end = seg_row(b_seg, 0)
        for j in range(1, seg_len):
            aj = seg_row(a_seg, j)
            h_end = aj * h_end + seg_row(b_seg, j)
            a_end = aj * a_end
        for k in (1, 2, 4):
            keep = row >= k
            a_sh = jnp.where(keep, pltpu.roll(a_end, k, 0), 1.0)
            h_sh = jnp.where(keep, pltpu.roll(h_end, k, 0), 0.0)
            h_end = a_end * h_sh + h_end
            a_end = a_end * a_sh
        ends = h_end + a_end * state["carry"]
        hj = jnp.where(row >= 1, pltpu.roll(ends, 1, 0), state["carry"])
        state["carry"] = jnp.broadcast_to(ends[SUBLANES - 1:SUBLANES, :], (SUBLANES, LRU_WIDTH))
        for j in range(seg_len):
            hj = seg_row(a_seg, j) * hj + seg_row(b_seg, j)
            for sl in range(n_slab):
                h_seg[n * n_slab + sl, pl.ds(j, SUBLANES, stride=pitch), :] = hj[:, sl * LANES:(sl + 1) * LANES]
        yield

        h_nat = jnp.concatenate(
            [jnp.concatenate([h_seg[n * n_slab + sl, s * pitch:s * pitch + seg_len, :]
                              for s in range(SUBLANES)], axis=0) for sl in range(n_slab)], axis=-1)
        y_main = (h_nat * _gelu_tanh(u_gate)).astype(BF16)
        y = (_dot(y_main, wout_ref[:MIX_WIDTH, :])
             + _dot_tn(ymt_buf[:, rows], wout_ref[MIX_WIDTH:, :]))
        o_ref[rows, :] = x + _rmsnorm(y, gpost_ref[...])

    _interleave([part(n) for n in range(MIX_SPLIT)])
    hcarry[...] = state["carry"]


def _rglru_layer(x, layer, j, p):
    bsz, seq, _ = x.shape
    tm = TM_MIX
    n_in = 2 * LRU_WIDTH + MEM_WIDTH
    in_specs = [
        pl.BlockSpec((None, tm, D_MODEL), lambda b, t: (b, t, 0)),
        pl.BlockSpec((None, 1, D_MODEL), lambda b, t: (layer, 0, 0)),
        pl.BlockSpec((None, D_MODEL, n_in), lambda b, t: (j, 0, 0)),
        pl.BlockSpec((None, LRU_CONV, LRU_WIDTH), lambda b, t: (j, 0, 0)),
        pl.BlockSpec((None, 1, LRU_WIDTH), lambda b, t: (j, 0, 0)),
        pl.BlockSpec((None, GATE_CHUNKS, MXU_DIM, 2 * MXU_DIM), lambda b, t: (j, 0, 0, 0)),
        pl.BlockSpec((None, GATE_CHUNKS, 1, 2 * MXU_DIM), lambda b, t: (j, 0, 0, 0)),
        pl.BlockSpec((None, 1, LRU_WIDTH), lambda b, t: (j, 0, 0)),
        pl.BlockSpec((None, None, MEM_LEN, MKPAD_WIDTH), lambda b, t: (layer, b, 0, 0)),
        pl.BlockSpec((None, None, MEM_WIDTH, MEM_LEN), lambda b, t: (layer, b, 0, 0)),
        pl.BlockSpec((None, D_MODEL, D_MODEL), lambda b, t: (layer, 0, 0)),
        pl.BlockSpec((None, 1, D_MODEL), lambda b, t: (layer, 0, 0)),
    ]
    cast_in, cast_out, cast_shape = _ffn_weight_cast_specs(layer, bsz, seq // tm)
    return pl.pallas_call(
        _rglru_kernel,
        grid=(bsz, seq // tm),
        in_specs=in_specs + cast_in,
        out_specs=[pl.BlockSpec((None, tm, D_MODEL), lambda b, t: (b, t, 0))] + cast_out,
        out_shape=[jax.ShapeDtypeStruct(x.shape, F32)] + cast_shape,
        scratch_shapes=[
            pltpu.VMEM((LRU_WIDTH // LANES, SUBLANES + tm, LANES), F32),
            pltpu.VMEM((MIX_SPLIT * LRU_WIDTH // LANES, tm // MIX_SPLIT + SUBLANES * SEG_GAP, LANES), F32),
            pltpu.VMEM((MIX_SPLIT * LRU_WIDTH // LANES, tm // MIX_SPLIT + SUBLANES * SEG_GAP, LANES), F32),
            pltpu.VMEM((MIX_SPLIT * LRU_WIDTH // LANES, tm // MIX_SPLIT + SUBLANES * SEG_GAP, LANES), F32),
            pltpu.VMEM((SUBLANES, LRU_WIDTH), F32),
            pltpu.VMEM((MEM_WIDTH, tm), BF16),
        ],
        compiler_params=pltpu.CompilerParams(
            dimension_semantics=("arbitrary", "arbitrary"), vmem_limit_bytes=VMEM_LIMIT_BYTES),
        name=f"rglru_mixer_{layer}",
    )(x, p["g_mix_pre"], p["w_in_a"], p["w_conv_a"], p["b_conv_a"], p["w_gate"], p["b_gate"],
      p["lru_lambda"], p["mk_pad"], p["mv_t"], p["w_mix_out"], p["g_mix_post"],
      p["w_ffn_up"], p["w_ffn_down"])


def _swa_kernel(sinks_ref, x_ref, gpre_ref, win_ref, kp_ref, kc_ref, vtp_ref, vtc_ref,
                mk_ref, mvt_ref, wout_ref, gpost_ref, wupf_ref, wdnf_ref,
                o_ref, wupb_ref, wdnb_ref, k_buf, vt_buf, bias_buf, yt_buf):
    wupb_ref[...] = wupf_ref[...].astype(BF16)
    wdnb_ref[...] = wdnf_ref[...].astype(BF16)
    tm = x_ref.shape[0]
    band_keys = 2 * WINDOW

    @pl.when((pl.program_id(0) == 0) & (pl.program_id(1) == 0))
    def _():
        kj = lax.broadcasted_iota(jnp.int32, (band_keys, WINDOW), 0)
        qi = lax.broadcasted_iota(jnp.int32, (band_keys, WINDOW), 1)
        dist = qi + WINDOW - kj
        in_window = (dist >= 0) & (dist < WINDOW)
        dist_f = dist.astype(F32)
        for hd in range(SWA_HEADS):
            bias_buf[hd] = jnp.where(in_window, -SLOPES[hd] * dist_f, -jnp.inf)

    k_buf[0:WINDOW, :] = kp_ref[...]
    k_buf[WINDOW:, :] = kc_ref[...]
    vt_buf[:, 0:WINDOW] = vtp_ref[...]
    vt_buf[:, WINDOW:] = vtc_ref[...]

    kj = lax.broadcasted_iota(jnp.int32, (band_keys, WINDOW), 0)
    no_prev = jnp.where(kj < jnp.where(pl.program_id(1) == 0, WINDOW, 0), -jnp.inf, 0.0)
    sub = tm // SWA_SPLIT

    def part(n):
        r0 = n * sub
        rows = slice(r0, r0 + sub)
        x = x_ref[rows, :]
        h = _rmsnorm(x, gpre_ref[...]).astype(BF16)
        q16 = (_dot(h, win_ref[...]) * QK_SCALE).astype(BF16)
        yield

        units = [(hd, qb) for hd in range(SWA_HEADS) for qb in range(r0 // WINDOW, (r0 + sub) // WINDOW)]
        s_ts = []
        for hd, qb in units:
            if (qb - r0 // WINDOW) % 2:
                continue
            g = hd // SWA_GROUP
            ktile = (hd % HEADS_PER_TILE) * SWA_KV_HEADS + g
            qtile = hd // HEADS_PER_TILE
            pair = _dot_nt(
                k_buf[qb * WINDOW:(qb + 1) * WINDOW + band_keys, ktile * LANES:(ktile + 1) * LANES],
                q16[qb * WINDOW - r0:(qb + 2) * WINDOW - r0, qtile * LANES:(qtile + 1) * LANES])
            s_ts.append(pair[:band_keys, :WINDOW])
            s_ts.append(pair[WINDOW:, WINDOW:])
        ms_ts = _memory_scores_t(q16, MIX_WIDTH // LANES, mk_ref)
        yield

        p_ts = []
        for (hd, qb), s_t in zip(units, s_ts):
            s_t = s_t + bias_buf[hd]
            if qb == 0:
                s_t = s_t + no_prev
            p_ts.append(_softmax_t(s_t, jnp.full((1, WINDOW), sinks_ref[hd], F32)))
        mp_ts = [_softmax_t(s_t) for s_t in ms_ts]
        yield

        for (hd, qb), p_t in zip(units, p_ts):
            g = hd // SWA_GROUP
            o_t = _dot(vt_buf[g * HEAD_DIM:(g + 1) * HEAD_DIM, qb * WINDOW:qb * WINDOW + band_keys], p_t)
            yt_buf[hd * HEAD_DIM:(hd + 1) * HEAD_DIM, qb * WINDOW:(qb + 1) * WINDOW] = o_t.astype(BF16)
        _memory_values_t(mp_ts, mvt_ref, yt_buf, MIX_WIDTH, rows)
        yield

        y = _dot_tn(yt_buf[:, rows], wout_ref[...])
        o_ref[rows, :] = x + _rmsnorm(y, gpost_ref[...])

    _interleave([part(n) for n in range(SWA_SPLIT)])


def _swa_layer(x, kpad, vt, layer, j, p):
    bsz, seq, _ = x.shape
    tm = TM_MIX
    per = tm // WINDOW
    prev_blk = lambda t: jnp.maximum(t * per - 1, 0)
    in_specs = [
        pl.BlockSpec(memory_space=pltpu.SMEM),
        pl.BlockSpec((None, tm, D_MODEL), lambda b, t: (b, t, 0)),
        pl.BlockSpec((None, 1, D_MODEL), lambda b, t: (layer, 0, 0)),
        pl.BlockSpec((None, D_MODEL, D_MODEL), lambda b, t: (j, 0, 0)),
        pl.BlockSpec((None, WINDOW, KPAD_WIDTH), lambda b, t: (b, prev_blk(t), 0)),
        pl.BlockSpec((None, tm, KPAD_WIDTH), lambda b, t: (b, t, 0)),
        pl.BlockSpec((None, KV_WIDTH, WINDOW), lambda b, t: (b, 0, prev_blk(t))),
        pl.BlockSpec((None, KV_WIDTH, tm), lambda b, t: (b, 0, t)),
        pl.BlockSpec((None, None, MEM_LEN, MKPAD_WIDTH), lambda b, t: (layer, b, 0, 0)),
        pl.BlockSpec((None, None, MEM_WIDTH, MEM_LEN), lambda b, t: (layer, b, 0, 0)),
        pl.BlockSpec((None, D_MODEL, D_MODEL), lambda b, t: (layer, 0, 0)),
        pl.BlockSpec((None, 1, D_MODEL), lambda b, t: (layer, 0, 0)),
    ]
    cast_in, cast_out, cast_shape = _ffn_weight_cast_specs(layer, bsz, seq // tm)
    return pl.pallas_call(
        _swa_kernel,
        grid=(bsz, seq // tm),
        in_specs=in_specs + cast_in,
        out_specs=[pl.BlockSpec((None, tm, D_MODEL), lambda b, t: (b, t, 0))] + cast_out,
        out_shape=[jax.ShapeDtypeStruct(x.shape, F32)] + cast_shape,
        scratch_shapes=[
            pltpu.VMEM((WINDOW + tm, KPAD_WIDTH), BF16),
            pltpu.VMEM((KV_WIDTH, WINDOW + tm), BF16),
            pltpu.VMEM((SWA_HEADS, 2 * WINDOW, WINDOW), F32),
            pltpu.VMEM((D_MODEL, tm), BF16),
        ],
        compiler_params=pltpu.CompilerParams(
            dimension_semantics=("arbitrary", "arbitrary"), vmem_limit_bytes=VMEM_LIMIT_BYTES),
        name=f"swa_mixer_{layer}",
    )(p["sinks_b"][j], x, p["g_mix_pre"], p["w_in_b"], kpad, kpad, vt, vt, p["mk_pad"], p["mv_t"],
      p["w_mix_out"], p["g_mix_post"], p["w_ffn_up"], p["w_ffn_down"])


def _ffn_kernel(x_ref, gpre_ref, wup_ref, wconv_ref, bconv_ref, wdown_ref, gpost_ref, *rest,
                emit_kv):
    if emit_kv:
        gkv_ref, wk2_ref, wvt_ref, o_ref, kpad_ref, vt_ref, ubuf, act_buf = rest
    else:
        o_ref, ubuf, act_buf = rest
    tm = x_ref.shape[0]
    pad = SUBLANES

    @pl.when(pl.program_id(1) == 0)
    def _():
        ubuf[:, 0:pad, :] = jnp.zeros((ubuf.shape[0], pad, LANES), F32)

    x = x_ref[...]
    h = _rmsnorm(x, gpre_ref[...]).astype(BF16)

    def conv(col0):
        u = _dot(h, wup_ref[:, col0:col0 + FFN_CHUNK])
        return _causal_dwconv(u, ubuf, wconv_ref, bconv_ref, col0)

    for c in range(FFN_CHUNKS):
        gate = conv(c * FFN_CHUNK)
        val = conv(D_FF + c * FFN_CHUNK)
        act_buf[:, c * FFN_CHUNK:(c + 1) * FFN_CHUNK] = (_gelu_tanh(gate) * val).astype(BF16)

    y = _dot(act_buf[...], wdown_ref[...])
    out = x + _rmsnorm(y, gpost_ref[...])
    o_ref[...] = out
    if emit_kv:
        _shared_kv(out, gkv_ref, wk2_ref, wvt_ref, kpad_ref, vt_ref)


def _ffn_layer(x, layer, p, w_up_b, w_down_b, kv_params=None):
    bsz, seq, _ = x.shape
    tm = TM_FFN
    emit_kv = kv_params is not None
    out_specs = [pl.BlockSpec((None, tm, D_MODEL), lambda b, t: (b, t, 0))]
    out_shape = [jax.ShapeDtypeStruct(x.shape, F32)]
    kv_specs = []
    if emit_kv:
        kv_specs = [
            pl.BlockSpec((1, D_MODEL), lambda b, t: (0, 0)),
            pl.BlockSpec((D_MODEL, 2 * KV_WIDTH), lambda b, t: (0, 0)),
            pl.BlockSpec((KV_WIDTH, D_MODEL), lambda b, t: (0, 0)),
        ]
        out_specs += [
            pl.BlockSpec((None, tm, KPAD_WIDTH), lambda b, t: (b, t, 0)),
            pl.BlockSpec((None, KV_WIDTH, tm), lambda b, t: (b, 0, t)),
        ]
        out_shape += [
            jax.ShapeDtypeStruct((bsz, seq, KPAD_WIDTH), BF16),
            jax.ShapeDtypeStruct((bsz, KV_WIDTH, seq), BF16),
        ]
    in_specs = [
        pl.BlockSpec((None, tm, D_MODEL), lambda b, t: (b, t, 0)),
        pl.BlockSpec((None, 1, D_MODEL), lambda b, t: (layer, 0, 0)),
        pl.BlockSpec((D_MODEL, 2 * D_FF), lambda b, t: (0, 0),
                     pipeline_mode=pl.Buffered(1)),
        pl.BlockSpec((None, FFN_CONV, 2 * D_FF), lambda b, t: (layer, 0, 0)),
        pl.BlockSpec((None, 1, 2 * D_FF), lambda b, t: (layer, 0, 0)),
        pl.BlockSpec((D_FF, D_MODEL), lambda b, t: (0, 0),
                     pipeline_mode=pl.Buffered(1)),
        pl.BlockSpec((None, 1, D_MODEL), lambda b, t: (layer, 0, 0)),
    ] + kv_specs
    outs = pl.pallas_call(
        functools.partial(_ffn_kernel, emit_kv=emit_kv),
        grid=(bsz, seq // tm),
        in_specs=in_specs,
        out_specs=out_specs,
        out_shape=out_shape,
        scratch_shapes=[
            pltpu.VMEM((2 * D_FF // LANES, SUBLANES + tm, LANES), F32),
            pltpu.VMEM((tm, D_FF), BF16),
        ],
        compiler_params=pltpu.CompilerParams(
            dimension_semantics=("arbitrary", "arbitrary"), vmem_limit_bytes=VMEM_LIMIT_BYTES),
        name=f"ffn_{layer}",
    )(x, p["g_ffn_pre"], w_up_b, p["w_ffn_conv"], p["b_ffn_conv"], w_down_b,
      p["g_ffn_post"], *(kv_params or ()))
    return outs if emit_kv else outs[0]


def _block_diag_gates(w_r, w_i, b_r, b_i):
    na = w_r.shape[0]
    per = MXU_DIM // LRU_BLOCK

    def bd(w):
        w = w.reshape(na, GATE_CHUNKS, per, LRU_BLOCK, LRU_BLOCK)
        eye = jnp.eye(per, dtype=w.dtype)
        full = w[:, :, :, :, None, :] * eye[None, None, :, None, :, None]
        return full.reshape(na, GATE_CHUNKS, MXU_DIM, MXU_DIM)

    w = jnp.concatenate([bd(w_r), bd(w_i)], axis=-1).astype(BF16)
    b = jnp.concatenate([b_r.reshape(na, GATE_CHUNKS, 1, MXU_DIM),
                         b_i.reshape(na, GATE_CHUNKS, 1, MXU_DIM)], axis=-1)
    return w, b


def _swap_head_pairs(w_k):
    d = w_k.shape[0]
    return w_k.reshape(d, SWA_KV_HEADS // 2, 2, HEAD_DIM)[:, :, ::-1, :].reshape(d, KV_WIDTH)


@jax.jit
def kernel(x, mem, g_mix_pre, g_mix_post, g_ffn_pre, g_ffn_post, g_mem, w_mem_kv, w_mix_out,
           w_ffn_up, w_ffn_conv, b_ffn_conv, w_ffn_down, w_in_a, w_conv_a, b_conv_a,
           w_rg_r, b_rg_r, w_rg_i, b_rg_i, lru_lambda, w_in_b, sinks_b, g_kv, w_kv):
    row = lambda a: a.reshape(a.shape[0], 1, a.shape[-1])
    w_gate, b_gate = _block_diag_gates(w_rg_r, w_rg_i, b_rg_r, b_rg_i)
    p = {
        "g_mix_pre": row(g_mix_pre), "g_mix_post": row(g_mix_post),
        "g_ffn_pre": row(g_ffn_pre), "g_ffn_post": row(g_ffn_post),
        "w_mix_out": w_mix_out.astype(BF16),
        "w_ffn_up": w_ffn_up, "w_ffn_down": w_ffn_down,
        "w_ffn_conv": w_ffn_conv, "b_ffn_conv": row(b_ffn_conv),
        "w_in_a": w_in_a.astype(BF16), "w_conv_a": w_conv_a, "b_conv_a": row(b_conv_a),
        "w_gate": w_gate, "b_gate": b_gate, "lru_lambda": row(lru_lambda),
        "w_in_b": w_in_b.astype(BF16), "sinks_b": sinks_b,
    }
    p["mk_pad"], p["mv_t"] = _memkv(
        mem, row(g_mem), w_mem_kv[:, :, :MEM_WIDTH].astype(BF16),
        jnp.swapaxes(w_mem_kv[:, :, MEM_WIDTH:], 1, 2).astype(BF16))
    w_k = w_kv[:, :KV_WIDTH]
    kv_params = (g_kv.reshape(1, D_MODEL),
                 jnp.concatenate([w_k, _swap_head_pairs(w_k)], axis=-1).astype(BF16),
                 w_kv[:, KV_WIDTH:].T.astype(BF16))
    kpad = vt = None
    for layer in range(DEPTH):
        if layer < N_A_LAYERS:
            x, w_up_b, w_down_b = _rglru_layer(x, layer, layer, p)
        else:
            x, w_up_b, w_down_b = _swa_layer(x, kpad, vt, layer, layer - N_A_LAYERS, p)
        if layer == N_A_LAYERS - 1:
            x, kpad, vt = _ffn_layer(x, layer, p, w_up_b, w_down_b, kv_params)
        else:
            x = _ffn_layer(x, layer, p, w_up_b, w_down_b)
    return x
```

```python
import functools
import math

import jax
import jax.numpy as jnp
import numpy as np
from jax import lax
from jax.experimental import pallas as pl
from jax.experimental.pallas import tpu as pltpu

D_MODEL = 1024
DEPTH = 4
N_A_LAYERS = DEPTH // 2
HEAD_DIM = 64
MEM_LEN = 256
MEM_HEADS = 4
MEM_WIDTH = MEM_HEADS * HEAD_DIM
MIX_WIDTH = D_MODEL - MEM_WIDTH
LRU_WIDTH = MIX_WIDTH
LRU_BLOCK = 64
LRU_CONV = 4
LRU_C = 8.0
SWA_HEADS = MIX_WIDTH // HEAD_DIM
SWA_KV_HEADS = 4
SWA_GROUP = SWA_HEADS // SWA_KV_HEADS
KV_WIDTH = SWA_KV_HEADS * HEAD_DIM
WINDOW = 128
D_FF = 2816
FFN_CONV = 3
EPS = 1e-6

SUBLANES = 8
LANES = 128
MXU_DIM = 256
VMEM_LIMIT_BYTES = 56 * 1024 * 1024

HEADS_PER_TILE = LANES // HEAD_DIM
KPAD_WIDTH = 2 * SWA_KV_HEADS * LANES
MKPAD_WIDTH = MEM_HEADS * LANES

GATE_CHUNKS = LRU_WIDTH // MXU_DIM
FFN_CHUNK = 256
FFN_CHUNKS = D_FF // FFN_CHUNK

TM_MIX = 1024
MIX_SPLIT = 4
FFN_SPLIT = 1
SWA_SPLIT = 1
CAST_ROWS_UP = 32
CAST_ROWS_DOWN = 128
SEG_GAP = 4
TM_FFN = 512

BF16 = jnp.bfloat16
F32 = jnp.float32
QK_SCALE = HEAD_DIM ** -0.5


def _alibi_slopes(n):
    def pow2_slopes(m):
        start = 2.0 ** (-8.0 / m)
        return [start ** (i + 1) for i in range(m)]
    c = 2 ** int(math.floor(math.log2(n)))
    s = pow2_slopes(c)
    if c != n:
        s = s + pow2_slopes(2 * c)[0::2][: n - c]
    return [float(np.float32(v)) for v in s]


SLOPES = _alibi_slopes(SWA_HEADS)


def _rmsnorm(x, g):
    ms = jnp.mean(x * x, axis=-1, keepdims=True)
    return x * lax.rsqrt(ms + EPS) * g


def _gelu_tanh(x):
    c = math.sqrt(2.0 / math.pi)
    return 0.5 * x * (1.0 + jnp.tanh(c * (x + 0.044715 * (x * x * x))))


def _dot(a, b):
    return jnp.dot(a, b, preferred_element_type=F32)


def _dot_nt(a, b):
    return lax.dot_general(a, b, (((1,), (1,)), ((), ())), preferred_element_type=F32)


def _dot_tn(a, b):
    return lax.dot_general(a, b, (((0,), (0,)), ((), ())), preferred_element_type=F32)


def _half_tiles(k, parities):
    upper = lax.broadcasted_iota(jnp.int32, (k.shape[0], LANES), 1) >= HEAD_DIM
    tiles = []
    for i, par in enumerate(parities):
        t = k[:, i * LANES:(i + 1) * LANES]
        tiles.append(jnp.where(upper if par else ~upper, t, 0.0))
    return jnp.concatenate(tiles, axis=-1).astype(BF16)


def _softmax_t(s_t, extra=None):
    m = jnp.max(s_t, axis=0, keepdims=True)
    if extra is not None:
        m = jnp.maximum(m, extra)
    p = jnp.exp(s_t - m)
    d = jnp.sum(p, axis=0, keepdims=True)
    if extra is not None:
        d = d + jnp.exp(extra - m)
    return (p * (1.0 / d)).astype(BF16)


def _causal_dwconv(u, ubuf, w_ref, b_ref, col0, row0=0, last=True):
    m = u.shape[0]
    taps = w_ref.shape[0]
    base = SUBLANES + row0
    outs = []
    for i in range(u.shape[1] // LANES):
        cols = slice(col0 + i * LANES, col0 + (i + 1) * LANES)
        slab = col0 // LANES + i
        ui = u[:, i * LANES:(i + 1) * LANES]
        ubuf[slab, base:base + m, :] = ui
        out = ui * w_ref[taps - 1:taps, cols] + b_ref[:, cols]
        for k in range(taps - 1):
            shift = taps - 1 - k
            out = out + ubuf[slab, base - shift:base - shift + m, :] * w_ref[k:k + 1, cols]
        if last:
            ubuf[slab, 0:SUBLANES, :] = ui[m - SUBLANES:, :]
        outs.append(out)
    return jnp.concatenate(outs, axis=-1)


def _memory_scores_t(q16, tile0, mk_ref):
    s_ts = []
    for hh in range(MEM_HEADS):
        tile = tile0 + hh // HEADS_PER_TILE
        s_ts.append(_dot_nt(mk_ref[:, hh * LANES:(hh + 1) * LANES],
                            q16[:, tile * LANES:(tile + 1) * LANES]))
    return s_ts


def _memory_values_t(p_ts, mvt_ref, yt_ref, row0, cols):
    for hh in range(MEM_HEADS):
        o_t = _dot(mvt_ref[hh * HEAD_DIM:(hh + 1) * HEAD_DIM, :], p_ts[hh])
        yt_ref[row0 + hh * HEAD_DIM:row0 + (hh + 1) * HEAD_DIM, cols] = o_t.astype(BF16)


def _interleave(parts):
    live = list(parts)
    while live:
        nxt = []
        for part in live:
            try:
                next(part)
                nxt.append(part)
            except StopIteration:
                pass
        live = nxt


def _ffn_weight_cast_specs(layer, bsz, nt):
    steps = bsz * nt
    assert steps * CAST_ROWS_UP == D_MODEL and steps * CAST_ROWS_DOWN >= D_FF and D_FF % CAST_ROWS_DOWN == 0
    up_blk = lambda b, t: b * nt + t
    down_blk = lambda b, t: jnp.minimum(b * nt + t, D_FF // CAST_ROWS_DOWN - 1)
    in_specs = [pl.BlockSpec((None, CAST_ROWS_UP, 2 * D_FF), lambda b, t: (layer, up_blk(b, t), 0)),
                pl.BlockSpec((None, CAST_ROWS_DOWN, D_MODEL), lambda b, t: (layer, down_blk(b, t), 0))]
    out_specs = [pl.BlockSpec((CAST_ROWS_UP, 2 * D_FF), lambda b, t: (up_blk(b, t), 0)),
                 pl.BlockSpec((CAST_ROWS_DOWN, D_MODEL), lambda b, t: (down_blk(b, t), 0))]
    out_shape = [jax.ShapeDtypeStruct((D_MODEL, 2 * D_FF), BF16),
                 jax.ShapeDtypeStruct((D_FF, D_MODEL), BF16)]
    return in_specs, out_specs, out_shape


def _memkv_kernel(mem_ref, g_ref, wk_ref, wvt_ref, mk_ref, mvt_ref):
    m = mem_ref[...]
    ms = jnp.mean(m * m, axis=-1, keepdims=True)
    mn = m * lax.rsqrt(ms + EPS)
    for l in range(DEPTH):
        mnl = (mn * g_ref[l]).astype(BF16)
        k = _dot(mnl, wk_ref[l])
        k4 = jnp.concatenate([k[:, (hh // 2) * LANES:(hh // 2 + 1) * LANES]
                              for hh in range(MEM_HEADS)], axis=-1)
        mk_ref[l] = _half_tiles(k4, [hh % 2 for hh in range(MEM_HEADS)])
        mvt_ref[l] = _dot_nt(wvt_ref[l], mnl).astype(BF16)


def _memkv(mem, g_mem, wk_b, wvt_b):
    bsz = mem.shape[0]
    return pl.pallas_call(
        _memkv_kernel,
        grid=(bsz,),
        in_specs=[
            pl.BlockSpec((None, MEM_LEN, D_MODEL), lambda b: (b, 0, 0)),
            pl.BlockSpec((DEPTH, 1, D_MODEL), lambda b: (0, 0, 0)),
            pl.BlockSpec((DEPTH, D_MODEL, MEM_WIDTH), lambda b: (0, 0, 0)),
            pl.BlockSpec((DEPTH, MEM_WIDTH, D_MODEL), lambda b: (0, 0, 0)),
        ],
        out_specs=[
            pl.BlockSpec((DEPTH, None, MEM_LEN, MKPAD_WIDTH), lambda b: (0, b, 0, 0)),
            pl.BlockSpec((DEPTH, None, MEM_WIDTH, MEM_LEN), lambda b: (0, b, 0, 0)),
        ],
        out_shape=[
            jax.ShapeDtypeStruct((DEPTH, bsz, MEM_LEN, MKPAD_WIDTH), BF16),
            jax.ShapeDtypeStruct((DEPTH, bsz, MEM_WIDTH, MEM_LEN), BF16),
        ],
        compiler_params=pltpu.CompilerParams(dimension_semantics=("arbitrary",)),
        name="memkv",
    )(mem, g_mem, wk_b, wvt_b)


def _shared_kv(x, g_ref, wk2_ref, wvt_ref, kpad_ref, vt_ref, rows):
    h = _rmsnorm(x, g_ref[...]).astype(BF16)
    k2 = _dot(h, wk2_ref[...])
    nat = [k2[:, i * LANES:(i + 1) * LANES] for i in range(2)]
    swp = [k2[:, KV_WIDTH + i * LANES:KV_WIDTH + (i + 1) * LANES] for i in range(2)]
    lo = jnp.concatenate([nat[0], swp[0], nat[1], swp[1]], axis=-1)
    hi = jnp.concatenate([swp[0], nat[0], swp[1], nat[1]], axis=-1)
    kpad_ref[rows, :SWA_KV_HEADS * LANES] = _half_tiles(lo, [0] * SWA_KV_HEADS)
    kpad_ref[rows, SWA_KV_HEADS * LANES:] = _half_tiles(hi, [1] * SWA_KV_HEADS)
    vt_ref[:, rows] = _dot_nt(wvt_ref[...], h).astype(BF16)


def _rglru_kernel(x_ref, gpre_ref, win_ref, wconv_ref, bconv_ref, wgate_ref, bgate_ref,
                  lam_ref, mk_ref, mvt_ref, wout_ref, gpost_ref, wupf_ref, wdnf_ref,
                  o_ref, wupb_ref, wdnb_ref, ubuf, a_seg, b_seg, h_seg, hcarry, ymt_buf):
    wupb_ref[...] = wupf_ref[...].astype(BF16)
    wdnb_ref[...] = wdnf_ref[...].astype(BF16)
    tm = x_ref.shape[0]
    pad = SUBLANES
    n_slab = LRU_WIDTH // LANES
    seg_len = tm // MIX_SPLIT // SUBLANES
    pitch = seg_len + SEG_GAP

    @pl.when(pl.program_id(1) == 0)
    def _():
        ubuf[:, 0:pad, :] = jnp.zeros((ubuf.shape[0], pad, LANES), F32)
        hcarry[...] = jnp.zeros_like(hcarry)

    lam = lam_ref[...]
    softplus_neg_lam = jnp.maximum(-lam, 0.0) + jnp.log1p(jnp.exp(-jnp.abs(lam)))
    half_decay = (0.5 * LRU_C) * softplus_neg_lam
    row = lax.broadcasted_iota(jnp.int32, (SUBLANES, LRU_WIDTH), 0)
    state = {"carry": hcarry[...]}
    sub = tm // MIX_SPLIT

    def part(n):
        r0 = n * sub
        rows = slice(r0, r0 + sub)
        x = x_ref[rows, :]
        h = _rmsnorm(x, gpre_ref[...]).astype(BF16)
        proj = _dot(h, win_ref[...])
        u_gate = proj[:, :LRU_WIDTH]
        u_x = proj[:, LRU_WIDTH:2 * LRU_WIDTH]
        q16 = (proj[:, 2 * LRU_WIDTH:] * QK_SCALE).astype(BF16)
        s_ts = _memory_scores_t(q16, 0, mk_ref)
        yield

        xc = _causal_dwconv(u_x, ubuf, wconv_ref, bconv_ref, 0, r0, n == MIX_SPLIT - 1)
        xcb = xc.astype(BF16)
        gates = [_dot(xcb[:, c * MXU_DIM:(c + 1) * MXU_DIM], wgate_ref[c]) + bgate_ref[c]
                 for c in range(GATE_CHUNKS)]
        _memory_values_t([_softmax_t(s_t) for s_t in s_ts], mvt_ref, ymt_buf, 0, rows)
        yield

        t_r = jnp.concatenate([jnp.tanh(0.5 * g[:, :MXU_DIM]) for g in gates], axis=-1)
        t_i = jnp.concatenate([jnp.tanh(0.5 * g[:, MXU_DIM:]) for g in gates], axis=-1)
        nla = half_decay * t_r + half_decay
        a = jnp.exp(-nla)
        z = jnp.tanh(nla) * (a * a + 1.0)
        gain = jnp.where(z > 0.0, z * lax.rsqrt(z), 0.0)
        b = gain * ((0.5 * t_i + 0.5) * xc)

        for sl in range(n_slab):
            lanes = slice(sl * LANES, (sl + 1) * LANES)
            for s in range(SUBLANES):
                seg = slice(s * pitch, s * pitch + seg_len)
                src = slice(s * seg_len, (s + 1) * seg_len)
                a_seg[n * n_slab + sl, seg, :] = a[src, lanes]
                b_seg[n * n_slab + sl, seg, :] = b[src, lanes]

        def seg_row(ref, j):
            return jnp.concatenate(
                [ref[n * n_slab + sl, pl.ds(j, SUBLANES, stride=pitch), :] for sl in range(n_slab)], axis=-1)

        a_end = seg_row(a_seg, 0)
        h_end = seg_row(b_seg, 0)
        for j in range(1, seg_len):
            aj = seg_row(a_seg, j)
            h_end = aj * h_end + seg_row(b_seg, j)
            a_end = aj * a_end
        for k in (1, 2, 4):
            keep = row >= k
            a_sh = jnp.where(keep, pltpu.roll(a_end, k, 0), 1.0)
            h_sh = jnp.where(keep, pltpu.roll(h_end, k, 0), 0.0)
            h_end = a_end * h_sh + h_end
            a_end = a_end * a_sh
        ends = h_end + a_end * state["carry"]
        hj = jnp.where(row >= 1, pltpu.roll(ends, 1, 0), state["carry"])
        state["carry"] = jnp.broadcast_to(ends[SUBLANES - 1:SUBLANES, :], (SUBLANES, LRU_WIDTH))
        for j in range(seg_len):
            hj = seg_row(a_seg, j) * hj + seg_row(b_seg, j)
            for sl in range(n_slab):
                h_seg[n * n_slab + sl, pl.ds(j, SUBLANES, stride=pitch), :] = hj[:, sl * LANES:(sl + 1) * LANES]
        yield

        h_nat = jnp.concatenate(
            [jnp.concatenate([h_seg[n * n_slab + sl, s * pitch:s * pitch + seg_len, :]
                              for s in range(SUBLANES)], axis=0) for sl in range(n_slab)], axis=-1)
        y_main = (h_nat * _gelu_tanh(u_gate)).astype(BF16)
        y = (_dot(y_main, wout_ref[:MIX_WIDTH, :])
             + _dot_tn(ymt_buf[:, rows], wout_ref[MIX_WIDTH:, :]))
        o_ref[rows, :] = x + _rmsnorm(y, gpost_ref[...])

    _interleave([part(n) for n in range(MIX_SPLIT)])
    hcarry[...] = state["carry"]


def _rglru_layer(x, layer, j, p):
    bsz, seq, _ = x.shape
    tm = TM_MIX
    n_in = 2 * LRU_WIDTH + MEM_WIDTH
    in_specs = [
        pl.BlockSpec((None, tm, D_MODEL), lambda b, t: (b, t, 0)),
        pl.BlockSpec((None, 1, D_MODEL), lambda b, t: (layer, 0, 0)),
        pl.BlockSpec((None, D_MODEL, n_in), lambda b, t: (j, 0, 0)),
        pl.BlockSpec((None, LRU_CONV, LRU_WIDTH), lambda b, t: (j, 0, 0)),
        pl.BlockSpec((None, 1, LRU_WIDTH), lambda b, t: (j, 0, 0)),
        pl.BlockSpec((None, GATE_CHUNKS, MXU_DIM, 2 * MXU_DIM), lambda b, t: (j, 0, 0, 0)),
        pl.BlockSpec((None, GATE_CHUNKS, 1, 2 * MXU_DIM), lambda b, t: (j, 0, 0, 0)),
        pl.BlockSpec((None, 1, LRU_WIDTH), lambda b, t: (j, 0, 0)),
        pl.BlockSpec((None, None, MEM_LEN, MKPAD_WIDTH), lambda b, t: (layer, b, 0, 0)),
        pl.BlockSpec((None, None, MEM_WIDTH, MEM_LEN), lambda b, t: (layer, b, 0, 0)),
        pl.BlockSpec((None, D_MODEL, D_MODEL), lambda b, t: (layer, 0, 0)),
        pl.BlockSpec((None, 1, D_MODEL), lambda b, t: (layer, 0, 0)),
    ]
    cast_in, cast_out, cast_shape = _ffn_weight_cast_specs(layer, bsz, seq // tm)
    return pl.pallas_call(
        _rglru_kernel,
        grid=(bsz, seq // tm),
        in_specs=in_specs + cast_in,
        out_specs=[pl.BlockSpec((None, tm, D_MODEL), lambda b, t: (b, t, 0))] + cast_out,
        out_shape=[jax.ShapeDtypeStruct(x.shape, F32)] + cast_shape,
        scratch_shapes=[
            pltpu.VMEM((LRU_WIDTH // LANES, SUBLANES + tm, LANES), F32),
            pltpu.VMEM((MIX_SPLIT * LRU_WIDTH // LANES, tm // MIX_SPLIT + SUBLANES * SEG_GAP, LANES), F32),
            pltpu.VMEM((MIX_SPLIT * LRU_WIDTH // LANES, tm // MIX_SPLIT + SUBLANES * SEG_GAP, LANES), F32),
            pltpu.VMEM((MIX_SPLIT * LRU_WIDTH // LANES, tm // MIX_SPLIT + SUBLANES * SEG_GAP, LANES), F32),
            pltpu.VMEM((SUBLANES, LRU_WIDTH), F32),
            pltpu.VMEM((MEM_WIDTH, tm), BF16),
        ],
        compiler_params=pltpu.CompilerParams(
            dimension_semantics=("arbitrary", "arbitrary"), vmem_limit_bytes=VMEM_LIMIT_BYTES),
        name=f"rglru_mixer_{layer}",
    )(x, p["g_mix_pre"], p["w_in_a"], p["w_conv_a"], p["b_conv_a"], p["w_gate"], p["b_gate"],
      p["lru_lambda"], p["mk_pad"], p["mv_t"], p["w_mix_out"], p["g_mix_post"],
      p["w_ffn_up"], p["w_ffn_down"])


def _swa_kernel(sinks_ref, x_ref, gpre_ref, win_ref, kp_ref, kc_ref, vtp_ref, vtc_ref,
                mk_ref, mvt_ref, wout_ref, gpost_ref, wupf_ref, wdnf_ref,
                o_ref, wupb_ref, wdnb_ref, k_buf, vt_buf, bias_buf, yt_buf):
    wupb_ref[...] = wupf_ref[...].astype(BF16)
    wdnb_ref[...] = wdnf_ref[...].astype(BF16)
    tm = x_ref.shape[0]
    band_keys = 2 * WINDOW

    @pl.when((pl.program_id(0) == 0) & (pl.program_id(1) == 0))
    def _():
        kj = lax.broadcasted_iota(jnp.int32, (band_keys, WINDOW), 0)
        qi = lax.broadcasted_iota(jnp.int32, (band_keys, WINDOW), 1)
        dist = qi + WINDOW - kj
        in_window = (dist >= 0) & (dist < WINDOW)
        dist_f = dist.astype(F32)
        for hd in range(SWA_HEADS):
            bias_buf[hd] = jnp.where(in_window, -SLOPES[hd] * dist_f, -jnp.inf)

    k_buf[0:WINDOW, :] = kp_ref[...]
    k_buf[WINDOW:, :] = kc_ref[...]
    vt_buf[:, 0:WINDOW] = vtp_ref[...]
    vt_buf[:, WINDOW:] = vtc_ref[...]

    kj = lax.broadcasted_iota(jnp.int32, (band_keys, WINDOW), 0)
    no_prev = jnp.where(kj < jnp.where(pl.program_id(1) == 0, WINDOW, 0), -jnp.inf, 0.0)
    sub = tm // SWA_SPLIT

    def part(n):
        r0 = n * sub
        rows = slice(r0, r0 + sub)
        x = x_ref[rows, :]
        h = _rmsnorm(x, gpre_ref[...]).astype(BF16)
        q16 = (_dot(h, win_ref[...]) * QK_SCALE).astype(BF16)
        yield

        units = [(hd, qb) for hd in range(SWA_HEADS) for qb in range(r0 // WINDOW, (r0 + sub) // WINDOW)]
        s_ts = []
        for hd, qb in units:
            if (qb - r0 // WINDOW) % 2:
                continue
            g = hd // SWA_GROUP
            ktile = (hd % HEADS_PER_TILE) * SWA_KV_HEADS + g
            qtile = hd // HEADS_PER_TILE
            pair = _dot_nt(
                k_buf[qb * WINDOW:(qb + 1) * WINDOW + band_keys, ktile * LANES:(ktile + 1) * LANES],
                q16[qb * WINDOW - r0:(qb + 2) * WINDOW - r0, qtile * LANES:(qtile + 1) * LANES])
            s_ts.append(pair[:band_keys, :WINDOW])
            s_ts.append(pair[WINDOW:, WINDOW:])
        ms_ts = _memory_scores_t(q16, MIX_WIDTH // LANES, mk_ref)
        yield

        p_ts = []
        for (hd, qb), s_t in zip(units, s_ts):
            s_t = s_t + bias_buf[hd]
            if qb == 0:
                s_t = s_t + no_prev
            p_ts.append(_softmax_t(s_t, jnp.full((1, WINDOW), sinks_ref[hd], F32)))
        mp_ts = [_softmax_t(s_t) for s_t in ms_ts]
        yield

        for (hd, qb), p_t in zip(units, p_ts):
            g = hd // SWA_GROUP
            o_t = _dot(vt_buf[g * HEAD_DIM:(g + 1) * HEAD_DIM, qb * WINDOW:qb * WINDOW + band_keys], p_t)
            yt_buf[hd * HEAD_DIM:(hd + 1) * HEAD_DIM, qb * WINDOW:(qb + 1) * WINDOW] = o_t.astype(BF16)
        _memory_values_t(mp_ts, mvt_ref, yt_buf, MIX_WIDTH, rows)
        yield

        y = _dot_tn(yt_buf[:, rows], wout_ref[...])
        o_ref[rows, :] = x + _rmsnorm(y, gpost_ref[...])

    _interleave([part(n) for n in range(SWA_SPLIT)])


def _swa_layer(x, kpad, vt, layer, j, p):
    bsz, seq, _ = x.shape
    tm = TM_MIX
    per = tm // WINDOW
    prev_blk = lambda t: jnp.maximum(t * per - 1, 0)
    in_specs = [
        pl.BlockSpec(memory_space=pltpu.SMEM),
        pl.BlockSpec((None, tm, D_MODEL), lambda b, t: (b, t, 0)),
        pl.BlockSpec((None, 1, D_MODEL), lambda b, t: (layer, 0, 0)),
        pl.BlockSpec((None, D_MODEL, D_MODEL), lambda b, t: (j, 0, 0)),
        pl.BlockSpec((None, WINDOW, KPAD_WIDTH), lambda b, t: (b, prev_blk(t), 0)),
        pl.BlockSpec((None, tm, KPAD_WIDTH), lambda b, t: (b, t, 0)),
        pl.BlockSpec((None, KV_WIDTH, WINDOW), lambda b, t: (b, 0, prev_blk(t))),
        pl.BlockSpec((None, KV_WIDTH, tm), lambda b, t: (b, 0, t)),
        pl.BlockSpec((None, None, MEM_LEN, MKPAD_WIDTH), lambda b, t: (layer, b, 0, 0)),
        pl.BlockSpec((None, None, MEM_WIDTH, MEM_LEN), lambda b, t: (layer, b, 0, 0)),
        pl.BlockSpec((None, D_MODEL, D_MODEL), lambda b, t: (layer, 0, 0)),
        pl.BlockSpec((None, 1, D_MODEL), lambda b, t: (layer, 0, 0)),
    ]
    cast_in, cast_out, cast_shape = _ffn_weight_cast_specs(layer, bsz, seq // tm)
    return pl.pallas_call(
        _swa_kernel,
        grid=(bsz, seq // tm),
        in_specs=in_specs + cast_in,
        out_specs=[pl.BlockSpec((None, tm, D_MODEL), lambda b, t: (b, t, 0))] + cast_out,
        out_shape=[jax.ShapeDtypeStruct(x.shape, F32)] + cast_shape,
        scratch_shapes=[
            pltpu.VMEM((WINDOW + tm, KPAD_WIDTH), BF16),
            pltpu.VMEM((KV_WIDTH, WINDOW + tm), BF16),
            pltpu.VMEM((SWA_HEADS, 2 * WINDOW, WINDOW), F32),
            pltpu.VMEM((D_MODEL, tm), BF16),
        ],
        compiler_params=pltpu.CompilerParams(
            dimension_semantics=("arbitrary", "arbitrary"), vmem_limit_bytes=VMEM_LIMIT_BYTES),
        name=f"swa_mixer_{layer}",
    )(p["sinks_b"][j], x, p["g_mix_pre"], p["w_in_b"], kpad, kpad, vt, vt, p["mk_pad"], p["mv_t"],
      p["w_mix_out"], p["g_mix_post"], p["w_ffn_up"], p["w_ffn_down"])


def _ffn_kernel(x_ref, gpre_ref, wup_ref, wconv_ref, bconv_ref, wdown_ref, gpost_ref, *rest,
                emit_kv):
    if emit_kv:
        gkv_ref, wk2_ref, wvt_ref, o_ref, kpad_ref, vt_ref, ubuf, act_buf = rest
    else:
        o_ref, ubuf, act_buf = rest
    tm = x_ref.shape[0]
    pad = SUBLANES

    @pl.when(pl.program_id(1) == 0)
    def _():
        ubuf[:, 0:pad, :] = jnp.zeros((ubuf.shape[0], pad, LANES), F32)

    sub = tm // FFN_SPLIT

    def part(n):
        r0 = n * sub
        rows = slice(r0, r0 + sub)
        x = x_ref[rows, :]
        h = _rmsnorm(x, gpre_ref[...]).astype(BF16)
        yield

        def conv(col0):
            u = _dot(h, wup_ref[:, col0:col0 + FFN_CHUNK])
            return _causal_dwconv(u, ubuf, wconv_ref, bconv_ref, col0, r0, n == FFN_SPLIT - 1)

        for c in range(FFN_CHUNKS):
            gate = conv(c * FFN_CHUNK)
            val = conv(D_FF + c * FFN_CHUNK)
            act_buf[rows, c * FFN_CHUNK:(c + 1) * FFN_CHUNK] = (_gelu_tanh(gate) * val).astype(BF16)
        yield

        y = _dot(act_buf[rows, :], wdown_ref[...])
        out = x + _rmsnorm(y, gpost_ref[...])
        o_ref[rows, :] = out
        if emit_kv:
            _shared_kv(out, gkv_ref, wk2_ref, wvt_ref, kpad_ref, vt_ref, rows)

    _interleave([part(n) for n in range(FFN_SPLIT)])


def _ffn_layer(x, layer, p, w_up_b, w_down_b, kv_params=None):
    bsz, seq, _ = x.shape
    tm = TM_FFN
    emit_kv = kv_params is not None
    out_specs = [pl.BlockSpec((None, tm, D_MODEL), lambda b, t: (b, t, 0))]
    out_shape = [jax.ShapeDtypeStruct(x.shape, F32)]
    kv_specs = []
    if emit_kv:
        kv_specs = [
            pl.BlockSpec((1, D_MODEL), lambda b, t: (0, 0)),
            pl.BlockSpec((D_MODEL, 2 * KV_WIDTH), lambda b, t: (0, 0)),
            pl.BlockSpec((KV_WIDTH, D_MODEL), lambda b, t: (0, 0)),
        ]
        out_specs += [
            pl.BlockSpec((None, tm, KPAD_WIDTH), lambda b, t: (b, t, 0)),
            pl.BlockSpec((None, KV_WIDTH, tm), lambda b, t: (b, 0, t)),
        ]
        out_shape += [
            jax.ShapeDtypeStruct((bsz, seq, KPAD_WIDTH), BF16),
            jax.ShapeDtypeStruct((bsz, KV_WIDTH, seq), BF16),
        ]
    in_specs = [
        pl.BlockSpec((None, tm, D_MODEL), lambda b, t: (b, t, 0)),
        pl.BlockSpec((None, 1, D_MODEL), lambda b, t: (layer, 0, 0)),
        pl.BlockSpec((D_MODEL, 2 * D_FF), lambda b, t: (0, 0),
                     pipeline_mode=pl.Buffered(1)),
        pl.BlockSpec((None, FFN_CONV, 2 * D_FF), lambda b, t: (layer, 0, 0)),
        pl.BlockSpec((None, 1, 2 * D_FF), lambda b, t: (layer, 0, 0)),
        pl.BlockSpec((D_FF, D_MODEL), lambda b, t: (0, 0),
                     pipeline_mode=pl.Buffered(1)),
        pl.BlockSpec((None, 1, D_MODEL), lambda b, t: (layer, 0, 0)),
    ] + kv_specs
    outs = pl.pallas_call(
        functools.partial(_ffn_kernel, emit_kv=emit_kv),
        grid=(bsz, seq // tm),
        in_specs=in_specs,
        out_specs=out_specs,
        out_shape=out_shape,
        scratch_shapes=[
            pltpu.VMEM((2 * D_FF // LANES, SUBLANES + tm, LANES), F32),
            pltpu.VMEM((tm, D_FF), BF16),
        ],
        compiler_params=pltpu.CompilerParams(
            dimension_semantics=("arbitrary", "arbitrary"), vmem_limit_bytes=VMEM_LIMIT_BYTES),
        name=f"ffn_{layer}",
    )(x, p["g_ffn_pre"], w_up_b, p["w_ffn_conv"], p["b_ffn_conv"], w_down_b,
      p["g_ffn_post"], *(kv_params or ()))
    return outs if emit_kv else outs[0]


def _block_diag_gates(w_r, w_i, b_r, b_i):
    na = w_r.shape[0]
    per = MXU_DIM // LRU_BLOCK

    def bd(w):
        w = w.reshape(na, GATE_CHUNKS, per, LRU_BLOCK, LRU_BLOCK)
        eye = jnp.eye(per, dtype=w.dtype)
        full = w[:, :, :, :, None, :] * eye[None, None, :, None, :, None]
        return full.reshape(na, GATE_CHUNKS, MXU_DIM, MXU_DIM)

    w = jnp.concatenate([bd(w_r), bd(w_i)], axis=-1).astype(BF16)
    b = jnp.concatenate([b_r.reshape(na, GATE_CHUNKS, 1, MXU_DIM),
                         b_i.reshape(na, GATE_CHUNKS, 1, MXU_DIM)], axis=-1)
    return w, b


def _swap_head_pairs(w_k):
    d = w_k.shape[0]
    return w_k.reshape(d, SWA_KV_HEADS // 2, 2, HEAD_DIM)[:, :, ::-1, :].reshape(d, KV_WIDTH)


@jax.jit
def kernel(x, mem, g_mix_pre, g_mix_post, g_ffn_pre, g_ffn_post, g_mem, w_mem_kv, w_mix_out,
           w_ffn_up, w_ffn_conv, b_ffn_conv, w_ffn_down, w_in_a, w_conv_a, b_conv_a,
           w_rg_r, b_rg_r, w_rg_i, b_rg_i, lru_lambda, w_in_b, sinks_b, g_kv, w_kv):
    row = lambda a: a.reshape(a.shape[0], 1, a.shape[-1])
    w_gate, b_gate = _block_diag_gates(w_rg_r, w_rg_i, b_rg_r, b_rg_i)
    p = {
        "g_mix_pre": row(g_mix_pre), "g_mix_post": row(g_mix_post),
        "g_ffn_pre": row(g_ffn_pre), "g_ffn_post": row(g_ffn_post),
        "w_mix_out": w_mix_out.astype(BF16),
        "w_ffn_up": w_ffn_up, "w_ffn_down": w_ffn_down,
        "w_ffn_conv": w_ffn_conv, "b_ffn_conv": row(b_ffn_conv),
        "w_in_a": w_in_a.astype(BF16), "w_conv_a": w_conv_a, "b_conv_a": row(b_conv_a),
        "w_gate": w_gate, "b_gate": b_gate, "lru_lambda": row(lru_lambda),
        "w_in_b": w_in_b.astype(BF16), "sinks_b": sinks_b,
    }
    p["mk_pad"], p["mv_t"] = _memkv(
        mem, row(g_mem), w_mem_kv[:, :, :MEM_WIDTH].astype(BF16),
        jnp.swapaxes(w_mem_kv[:, :, MEM_WIDTH:], 1, 2).astype(BF16))
    w_k = w_kv[:, :KV_WIDTH]
    kv_params = (g_kv.reshape(1, D_MODEL),
                 jnp.concatenate([w_k, _swap_head_pairs(w_k)], axis=-1).astype(BF16),
                 w_kv[:, KV_WIDTH:].T.astype(BF16))
    kpad = vt = None
    for layer in range(DEPTH):
        if layer < N_A_LAYERS:
            x, w_up_b, w_down_b = _rglru_layer(x, layer, layer, p)
        else:
            x, w_up_b, w_down_b = _swa_layer(x, kpad, vt, layer, layer - N_A_LAYERS, p)
        if layer == N_A_LAYERS - 1:
            x, kpad, vt = _ffn_layer(x, layer, p, w_up_b, w_down_b, kv_params)
        else:
            x = _ffn_layer(x, layer, p, w_up_b, w_down_b)
    return x
```

```python
import functools
import math

import jax
import jax.numpy as jnp
import numpy as np
from jax import lax
from jax.experimental import pallas as pl
from jax.experimental.pallas import tpu as pltpu

D_MODEL = 1024
DEPTH = 4
N_A_LAYERS = DEPTH // 2
HEAD_DIM = 64
MEM_LEN = 256
MEM_HEADS = 4
MEM_WIDTH = MEM_HEADS * HEAD_DIM
MIX_WIDTH = D_MODEL - MEM_WIDTH
LRU_WIDTH = MIX_WIDTH
LRU_BLOCK = 64
LRU_CONV = 4
LRU_C = 8.0
SWA_HEADS = MIX_WIDTH // HEAD_DIM
SWA_KV_HEADS = 4
SWA_GROUP = SWA_HEADS // SWA_KV_HEADS
KV_WIDTH = SWA_KV_HEADS * HEAD_DIM
WINDOW = 128
D_FF = 2816
FFN_CONV = 3
EPS = 1e-6

SUBLANES = 8
LANES = 128
MXU_DIM = 256
VMEM_LIMIT_BYTES = 56 * 1024 * 1024

HEADS_PER_TILE = LANES // HEAD_DIM
KPAD_WIDTH = 2 * SWA_KV_HEADS * LANES
MKPAD_WIDTH = MEM_HEADS * LANES

GATE_CHUNKS = LRU_WIDTH // MXU_DIM
FFN_CHUNK = 256
FFN_CHUNKS = D_FF // FFN_CHUNK

TM_MIX = 1024
MIX_SPLIT = 4
FFN_SPLIT = 1
SWA_SPLIT = 1
CAST_ROWS_UP = 32
CAST_ROWS_DOWN = 128
SEG_GAP = 4
TM_FFN = 1024
FFN_PASS = 512

BF16 = jnp.bfloat16
F32 = jnp.float32
QK_SCALE = HEAD_DIM ** -0.5


def _alibi_slopes(n):
    def pow2_slopes(m):
        start = 2.0 ** (-8.0 / m)
        return [start ** (i + 1) for i in range(m)]
    c = 2 ** int(math.floor(math.log2(n)))
    s = pow2_slopes(c)
    if c != n:
        s = s + pow2_slopes(2 * c)[0::2][: n - c]
    return [float(np.float32(v)) for v in s]


SLOPES = _alibi_slopes(SWA_HEADS)


def _rmsnorm(x, g):
    ms = jnp.mean(x * x, axis=-1, keepdims=True)
    return x * lax.rsqrt(ms + EPS) * g


def _gelu_tanh(x):
    c = math.sqrt(2.0 / math.pi)
    return 0.5 * x * (1.0 + jnp.tanh(c * (x + 0.044715 * (x * x * x))))


def _dot(a, b):
    return jnp.dot(a, b, preferred_element_type=F32)


def _dot_nt(a, b):
    return lax.dot_general(a, b, (((1,), (1,)), ((), ())), preferred_element_type=F32)


def _dot_tn(a, b):
    return lax.dot_general(a, b, (((0,), (0,)), ((), ())), preferred_element_type=F32)


def _half_tiles(k, parities):
    upper = lax.broadcasted_iota(jnp.int32, (k.shape[0], LANES), 1) >= HEAD_DIM
    tiles = []
    for i, par in enumerate(parities):
        t = k[:, i * LANES:(i + 1) * LANES]
        tiles.append(jnp.where(upper if par else ~upper, t, 0.0))
    return jnp.concatenate(tiles, axis=-1).astype(BF16)


def _softmax_t(s_t, extra=None):
    m = jnp.max(s_t, axis=0, keepdims=True)
    if extra is not None:
        m = jnp.maximum(m, extra)
    p = jnp.exp(s_t - m)
    d = jnp.sum(p, axis=0, keepdims=True)
    if extra is not None:
        d = d + jnp.exp(extra - m)
    return (p * (1.0 / d)).astype(BF16)


def _causal_dwconv(u, ubuf, w_ref, b_ref, col0, row0=0, last=True):
    m = u.shape[0]
    taps = w_ref.shape[0]
    base = SUBLANES + row0
    outs = []
    for i in range(u.shape[1] // LANES):
        cols = slice(col0 + i * LANES, col0 + (i + 1) * LANES)
        slab = col0 // LANES + i
        ui = u[:, i * LANES:(i + 1) * LANES]
        ubuf[slab, base:base + m, :] = ui
        out = ui * w_ref[taps - 1:taps, cols] + b_ref[:, cols]
        for k in range(taps - 1):
            shift = taps - 1 - k
            out = out + ubuf[slab, base - shift:base - shift + m, :] * w_ref[k:k + 1, cols]
        if last:
            ubuf[slab, 0:SUBLANES, :] = ui[m - SUBLANES:, :]
        outs.append(out)
    return jnp.concatenate(outs, axis=-1)


def _memory_scores_t(q16, tile0, mk_ref):
    s_ts = []
    for hh in range(MEM_HEADS):
        tile = tile0 + hh // HEADS_PER_TILE
        s_ts.append(_dot_nt(mk_ref[:, hh * LANES:(hh + 1) * LANES],
                            q16[:, tile * LANES:(tile + 1) * LANES]))
    return s_ts


def _memory_values_t(p_ts, mvt_ref, yt_ref, row0, cols):
    for hh in range(MEM_HEADS):
        o_t = _dot(mvt_ref[hh * HEAD_DIM:(hh + 1) * HEAD_DIM, :], p_ts[hh])
        yt_ref[row0 + hh * HEAD_DIM:row0 + (hh + 1) * HEAD_DIM, cols] = o_t.astype(BF16)


def _interleave(parts):
    live = list(parts)
    while live:
        nxt = []
        for part in live:
            try:
                next(part)
                nxt.append(part)
            except StopIteration:
                pass
        live = nxt


def _ffn_weight_cast_specs(layer, bsz, nt):
    steps = bsz * nt
    assert steps * CAST_ROWS_UP == D_MODEL and steps * CAST_ROWS_DOWN >= D_FF and D_FF % CAST_ROWS_DOWN == 0
    up_blk = lambda b, t: b * nt + t
    down_blk = lambda b, t: jnp.minimum(b * nt + t, D_FF // CAST_ROWS_DOWN - 1)
    in_specs = [pl.BlockSpec((None, CAST_ROWS_UP, 2 * D_FF), lambda b, t: (layer, up_blk(b, t), 0)),
                pl.BlockSpec((None, CAST_ROWS_DOWN, D_MODEL), lambda b, t: (layer, down_blk(b, t), 0))]
    out_specs = [pl.BlockSpec((CAST_ROWS_UP, 2 * D_FF), lambda b, t: (up_blk(b, t), 0)),
                 pl.BlockSpec((CAST_ROWS_DOWN, D_MODEL), lambda b, t: (down_blk(b, t), 0))]
    out_shape = [jax.ShapeDtypeStruct((D_MODEL, 2 * D_FF), BF16),
                 jax.ShapeDtypeStruct((D_FF, D_MODEL), BF16)]
    return in_specs, out_specs, out_shape


def _memkv_kernel(mem_ref, g_ref, wk_ref, wvt_ref, mk_ref, mvt_ref):
    m = mem_ref[...]
    ms = jnp.mean(m * m, axis=-1, keepdims=True)
    mn = m * lax.rsqrt(ms + EPS)
    for l in range(DEPTH):
        mnl = (mn * g_ref[l]).astype(BF16)
        k = _dot(mnl, wk_ref[l])
        k4 = jnp.concatenate([k[:, (hh // 2) * LANES:(hh // 2 + 1) * LANES]
                              for hh in range(MEM_HEADS)], axis=-1)
        mk_ref[l] = _half_tiles(k4, [hh % 2 for hh in range(MEM_HEADS)])
        mvt_ref[l] = _dot_nt(wvt_ref[l], mnl).astype(BF16)


def _memkv(mem, g_mem, wk_b, wvt_b):
    bsz = mem.shape[0]
    return pl.pallas_call(
        _memkv_kernel,
        grid=(bsz,),
        in_specs=[
            pl.BlockSpec((None, MEM_LEN, D_MODEL), lambda b: (b, 0, 0)),
            pl.BlockSpec((DEPTH, 1, D_MODEL), lambda b: (0, 0, 0)),
            pl.BlockSpec((DEPTH, D_MODEL, MEM_WIDTH), lambda b: (0, 0, 0)),
            pl.BlockSpec((DEPTH, MEM_WIDTH, D_MODEL), lambda b: (0, 0, 0)),
        ],
        out_specs=[
            pl.BlockSpec((DEPTH, None, MEM_LEN, MKPAD_WIDTH), lambda b: (0, b, 0, 0)),
            pl.BlockSpec((DEPTH, None, MEM_WIDTH, MEM_LEN), lambda b: (0, b, 0, 0)),
        ],
        out_shape=[
            jax.ShapeDtypeStruct((DEPTH, bsz, MEM_LEN, MKPAD_WIDTH), BF16),
            jax.ShapeDtypeStruct((DEPTH, bsz, MEM_WIDTH, MEM_LEN), BF16),
        ],
        compiler_params=pltpu.CompilerParams(dimension_semantics=("arbitrary",)),
        name="memkv",
    )(mem, g_mem, wk_b, wvt_b)


def _shared_kv(x, g_ref, wk2_ref, wvt_ref, kpad_ref, vt_ref, rows):
    h = _rmsnorm(x, g_ref[...]).astype(BF16)
    k2 = _dot(h, wk2_ref[...])
    nat = [k2[:, i * LANES:(i + 1) * LANES] for i in range(2)]
    swp = [k2[:, KV_WIDTH + i * LANES:KV_WIDTH + (i + 1) * LANES] for i in range(2)]
    lo = jnp.concatenate([nat[0], swp[0], nat[1], swp[1]], axis=-1)
    hi = jnp.concatenate([swp[0], nat[0], swp[1], nat[1]], axis=-1)
    kpad_ref[rows, :SWA_KV_HEADS * LANES] = _half_tiles(lo, [0] * SWA_KV_HEADS)
    kpad_ref[rows, SWA_KV_HEADS * LANES:] = _half_tiles(hi, [1] * SWA_KV_HEADS)
    vt_ref[:, rows] = _dot_nt(wvt_ref[...], h).astype(BF16)


def _rglru_kernel(x_ref, gpre_ref, win_ref, wconv_ref, bconv_ref, wgate_ref, bgate_ref,
                  lam_ref, mk_ref, mvt_ref, wout_ref, gpost_ref, wupf_ref, wdnf_ref,
                  o_ref, wupb_ref, wdnb_ref, ubuf, a_seg, b_seg, h_seg, hcarry, ymt_buf):
    wupb_ref[...] = wupf_ref[...].astype(BF16)
    wdnb_ref[...] = wdnf_ref[...].astype(BF16)
    tm = x_ref.shape[0]
    pad = SUBLANES
    n_slab = LRU_WIDTH // LANES
    seg_len = tm // MIX_SPLIT // SUBLANES
    pitch = seg_len + SEG_GAP

    @pl.when(pl.program_id(1) == 0)
    def _():
        ubuf[:, 0:pad, :] = jnp.zeros((ubuf.shape[0], pad, LANES), F32)
        hcarry[...] = jnp.zeros_like(hcarry)

    lam = lam_ref[...]
    softplus_neg_lam = jnp.maximum(-lam, 0.0) + jnp.log1p(jnp.exp(-jnp.abs(lam)))
    half_decay = (0.5 * LRU_C) * softplus_neg_lam
    row = lax.broadcasted_iota(jnp.int32, (SUBLANES, LRU_WIDTH), 0)
    state = {"carry": hcarry[...]}
    sub = tm // MIX_SPLIT

    def part(n):
        r0 = n * sub
        rows = slice(r0, r0 + sub)
        x = x_ref[rows, :]
        h = _rmsnorm(x, gpre_ref[...]).astype(BF16)
        proj = _dot(h, win_ref[...])
        u_gate = proj[:, :LRU_WIDTH]
        u_x = proj[:, LRU_WIDTH:2 * LRU_WIDTH]
        q16 = (proj[:, 2 * LRU_WIDTH:] * QK_SCALE).astype(BF16)
        s_ts = _memory_scores_t(q16, 0, mk_ref)
        yield

        xc = _causal_dwconv(u_x, ubuf, wconv_ref, bconv_ref, 0, r0, n == MIX_SPLIT - 1)
        xcb = xc.astype(BF16)
        gates = [_dot(xcb[:, c * MXU_DIM:(c + 1) * MXU_DIM], wgate_ref[c]) + bgate_ref[c]
                 for c in range(GATE_CHUNKS)]
        _memory_values_t([_softmax_t(s_t) for s_t in s_ts], mvt_ref, ymt_buf, 0, rows)
        yield

        t_r = jnp.concatenate([jnp.tanh(0.5 * g[:, :MXU_DIM]) for g in gates], axis=-1)
        t_i = jnp.concatenate([jnp.tanh(0.5 * g[:, MXU_DIM:]) for g in gates], axis=-1)
        nla = half_decay * t_r + half_decay
        a = jnp.exp(-nla)
        z = jnp.tanh(nla) * (a * a + 1.0)
        gain = jnp.where(z > 0.0, z * lax.rsqrt(z), 0.0)
        b = gain * ((0.5 * t_i + 0.5) * xc)

        for sl in range(n_slab):
            lanes = slice(sl * LANES, (sl + 1) * LANES)
            for s in range(SUBLANES):
                seg = slice(s * pitch, s * pitch + seg_len)
                src = slice(s * seg_len, (s + 1) * seg_len)
                a_seg[n * n_slab + sl, seg, :] = a[src, lanes]
                b_seg[n * n_slab + sl, seg, :] = b[src, lanes]

        def seg_row(ref, j):
            return jnp.concatenate(
                [ref[n * n_slab + sl, pl.ds(j, SUBLANES, stride=pitch), :] for sl in range(n_slab)], axis=-1)

        a_end = seg_row(a_seg, 0)
        h_end = seg_row(b_seg, 0)
        for j in range(1, seg_len):
            aj = seg_row(a_seg, j)
            h_end = aj * h_end + seg_row(b_seg, j)
            a_end = aj * a_end
        for k in (1, 2, 4):
            keep = row >= k
            a_sh = jnp.where(keep, pltpu.roll(a_end, k, 0), 1.0)
            h_sh = jnp.where(keep, pltpu.roll(h_end, k, 0), 0.0)
            h_end = a_end * h_sh + h_end
            a_end = a_end * a_sh
        ends = h_end + a_end * state["carry"]
        hj = jnp.where(row >= 1, pltpu.roll(ends, 1, 0), state["carry"])
        state["carry"] = jnp.broadcast_to(ends[SUBLANES - 1:SUBLANES, :], (SUBLANES, LRU_WIDTH))
        for j in range(seg_len):
            hj = seg_row(a_seg, j) * hj + seg_row(b_seg, j)
            for sl in range(n_slab):
                h_seg[n * n_slab + sl, pl.ds(j, SUBLANES, stride=pitch), :] = hj[:, sl * LANES:(sl + 1) * LANES]
        yield

        h_nat = jnp.concatenate(
            [jnp.concatenate([h_seg[n * n_slab + sl, s * pitch:s * pitch + seg_len, :]
                              for s in range(SUBLANES)], axis=0) for sl in range(n_slab)], axis=-1)
        y_main = (h_nat * _gelu_tanh(u_gate)).astype(BF16)
        y = (_dot(y_main, wout_ref[:MIX_WIDTH, :])
             + _dot_tn(ymt_buf[:, rows], wout_ref[MIX_WIDTH:, :]))
        o_ref[rows, :] = x + _rmsnorm(y, gpost_ref[...])

    _interleave([part(n) for n in range(MIX_SPLIT)])
    hcarry[...] = state["carry"]


def _rglru_layer(x, layer, j, p):
    bsz, seq, _ = x.shape
    tm = TM_MIX
    n_in = 2 * LRU_WIDTH + MEM_WIDTH
    in_specs = [
        pl.BlockSpec((None, tm, D_MODEL), lambda b, t: (b, t, 0)),
        pl.BlockSpec((None, 1, D_MODEL), lambda b, t: (layer, 0, 0)),
        pl.BlockSpec((None, D_MODEL, n_in), lambda b, t: (j, 0, 0)),
        pl.BlockSpec((None, LRU_CONV, LRU_WIDTH), lambda b, t: (j, 0, 0)),
        pl.BlockSpec((None, 1, LRU_WIDTH), lambda b, t: (j, 0, 0)),
        pl.BlockSpec((None, GATE_CHUNKS, MXU_DIM, 2 * MXU_DIM), lambda b, t: (j, 0, 0, 0)),
        pl.BlockSpec((None, GATE_CHUNKS, 1, 2 * MXU_DIM), lambda b, t: (j, 0, 0, 0)),
        pl.BlockSpec((None, 1, LRU_WIDTH), lambda b, t: (j, 0, 0)),
        pl.BlockSpec((None, None, MEM_LEN, MKPAD_WIDTH), lambda b, t: (layer, b, 0, 0)),
        pl.BlockSpec((None, None, MEM_WIDTH, MEM_LEN), lambda b, t: (layer, b, 0, 0)),
        pl.BlockSpec((None, D_MODEL, D_MODEL), lambda b, t: (layer, 0, 0)),
        pl.BlockSpec((None, 1, D_MODEL), lambda b, t: (layer, 0, 0)),
    ]
    cast_in, cast_out, cast_shape = _ffn_weight_cast_specs(layer, bsz, seq // tm)
    return pl.pallas_call(
        _rglru_kernel,
        grid=(bsz, seq // tm),
        in_specs=in_specs + cast_in,
        out_specs=[pl.BlockSpec((None, tm, D_MODEL), lambda b, t: (b, t, 0))] + cast_out,
        out_shape=[jax.ShapeDtypeStruct(x.shape, F32)] + cast_shape,
        scratch_shapes=[
            pltpu.VMEM((LRU_WIDTH // LANES, SUBLANES + tm, LANES), F32),
            pltpu.VMEM((MIX_SPLIT * LRU_WIDTH // LANES, tm // MIX_SPLIT + SUBLANES * SEG_GAP, LANES), F32),
            pltpu.VMEM((MIX_SPLIT * LRU_WIDTH // LANES, tm // MIX_SPLIT + SUBLANES * SEG_GAP, LANES), F32),
            pltpu.VMEM((MIX_SPLIT * LRU_WIDTH // LANES, tm // MIX_SPLIT + SUBLANES * SEG_GAP, LANES), F32),
            pltpu.VMEM((SUBLANES, LRU_WIDTH), F32),
            pltpu.VMEM((MEM_WIDTH, tm), BF16),
        ],
        compiler_params=pltpu.CompilerParams(
            dimension_semantics=("arbitrary", "arbitrary"), vmem_limit_bytes=VMEM_LIMIT_BYTES),
        name=f"rglru_mixer_{layer}",
    )(x, p["g_mix_pre"], p["w_in_a"], p["w_conv_a"], p["b_conv_a"], p["w_gate"], p["b_gate"],
      p["lru_lambda"], p["mk_pad"], p["mv_t"], p["w_mix_out"], p["g_mix_post"],
      p["w_ffn_up"], p["w_ffn_down"])


def _swa_kernel(sinks_ref, x_ref, gpre_ref, win_ref, kp_ref, kc_ref, vtp_ref, vtc_ref,
                mk_ref, mvt_ref, wout_ref, gpost_ref, wupf_ref, wdnf_ref,
                o_ref, wupb_ref, wdnb_ref, k_buf, vt_buf, bias_buf, yt_buf):
    wupb_ref[...] = wupf_ref[...].astype(BF16)
    wdnb_ref[...] = wdnf_ref[...].astype(BF16)
    tm = x_ref.shape[0]
    band_keys = 2 * WINDOW

    @pl.when((pl.program_id(0) == 0) & (pl.program_id(1) == 0))
    def _():
        kj = lax.broadcasted_iota(jnp.int32, (band_keys, WINDOW), 0)
        qi = lax.broadcasted_iota(jnp.int32, (band_keys, WINDOW), 1)
        dist = qi + WINDOW - kj
        in_window = (dist >= 0) & (dist < WINDOW)
        dist_f = dist.astype(F32)
        for hd in range(SWA_HEADS):
            bias_buf[hd] = jnp.where(in_window, -SLOPES[hd] * dist_f, -jnp.inf)

    k_buf[0:WINDOW, :] = kp_ref[...]
    k_buf[WINDOW:, :] = kc_ref[...]
    vt_buf[:, 0:WINDOW] = vtp_ref[...]
    vt_buf[:, WINDOW:] = vtc_ref[...]

    kj = lax.broadcasted_iota(jnp.int32, (band_keys, WINDOW), 0)
    no_prev = jnp.where(kj < jnp.where(pl.program_id(1) == 0, WINDOW, 0), -jnp.inf, 0.0)
    sub = tm // SWA_SPLIT

    def part(n):
        r0 = n * sub
        rows = slice(r0, r0 + sub)
        x = x_ref[rows, :]
        h = _rmsnorm(x, gpre_ref[...]).astype(BF16)
        q16 = (_dot(h, win_ref[...]) * QK_SCALE).astype(BF16)
        yield

        units = [(hd, qb) for hd in range(SWA_HEADS) for qb in range(r0 // WINDOW, (r0 + sub) // WINDOW)]
        s_ts = []
        for hd, qb in units:
            if (qb - r0 // WINDOW) % 2:
                continue
            g = hd // SWA_GROUP
            ktile = (hd % HEADS_PER_TILE) * SWA_KV_HEADS + g
            qtile = hd // HEADS_PER_TILE
            pair = _dot_nt(
                k_buf[qb * WINDOW:(qb + 1) * WINDOW + band_keys, ktile * LANES:(ktile + 1) * LANES],
                q16[qb * WINDOW - r0:(qb + 2) * WINDOW - r0, qtile * LANES:(qtile + 1) * LANES])
            s_ts.append(pair[:band_keys, :WINDOW])
            s_ts.append(pair[WINDOW:, WINDOW:])
        ms_ts = _memory_scores_t(q16, MIX_WIDTH // LANES, mk_ref)
        yield

        p_ts = []
        for (hd, qb), s_t in zip(units, s_ts):
            s_t = s_t + bias_buf[hd]
            if qb == 0:
                s_t = s_t + no_prev
            p_ts.append(_softmax_t(s_t, jnp.full((1, WINDOW), sinks_ref[hd], F32)))
        mp_ts = [_softmax_t(s_t) for s_t in ms_ts]
        yield

        for (hd, qb), p_t in zip(units, p_ts):
            g = hd // SWA_GROUP
            o_t = _dot(vt_buf[g * HEAD_DIM:(g + 1) * HEAD_DIM, qb * WINDOW:qb * WINDOW + band_keys], p_t)
            yt_buf[hd * HEAD_DIM:(hd + 1) * HEAD_DIM, qb * WINDOW:(qb + 1) * WINDOW] = o_t.astype(BF16)
        _memory_values_t(mp_ts, mvt_ref, yt_buf, MIX_WIDTH, rows)
        yield

        y = _dot_tn(yt_buf[:, rows], wout_ref[...])
        o_ref[rows, :] = x + _rmsnorm(y, gpost_ref[...])

    _interleave([part(n) for n in range(SWA_SPLIT)])


def _swa_layer(x, kpad, vt, layer, j, p):
    bsz, seq, _ = x.shape
    tm = TM_MIX
    per = tm // WINDOW
    prev_blk = lambda t: jnp.maximum(t * per - 1, 0)
    in_specs = [
        pl.BlockSpec(memory_space=pltpu.SMEM),
        pl.BlockSpec((None, tm, D_MODEL), lambda b, t: (b, t, 0)),
        pl.BlockSpec((None, 1, D_MODEL), lambda b, t: (layer, 0, 0)),
        pl.BlockSpec((None, D_MODEL, D_MODEL), lambda b, t: (j, 0, 0)),
        pl.BlockSpec((None, WINDOW, KPAD_WIDTH), lambda b, t: (b, prev_blk(t), 0)),
        pl.BlockSpec((None, tm, KPAD_WIDTH), lambda b, t: (b, t, 0)),
        pl.BlockSpec((None, KV_WIDTH, WINDOW), lambda b, t: (b, 0, prev_blk(t))),
        pl.BlockSpec((None, KV_WIDTH, tm), lambda b, t: (b, 0, t)),
        pl.BlockSpec((None, None, MEM_LEN, MKPAD_WIDTH), lambda b, t: (layer, b, 0, 0)),
        pl.BlockSpec((None, None, MEM_WIDTH, MEM_LEN), lambda b, t: (layer, b, 0, 0)),
        pl.BlockSpec((None, D_MODEL, D_MODEL), lambda b, t: (layer, 0, 0)),
        pl.BlockSpec((None, 1, D_MODEL), lambda b, t: (layer, 0, 0)),
    ]
    cast_in, cast_out, cast_shape = _ffn_weight_cast_specs(layer, bsz, seq // tm)
    return pl.pallas_call(
        _swa_kernel,
        grid=(bsz, seq // tm),
        in_specs=in_specs + cast_in,
        out_specs=[pl.BlockSpec((None, tm, D_MODEL), lambda b, t: (b, t, 0))] + cast_out,
        out_shape=[jax.ShapeDtypeStruct(x.shape, F32)] + cast_shape,
        scratch_shapes=[
            pltpu.VMEM((WINDOW + tm, KPAD_WIDTH), BF16),
            pltpu.VMEM((KV_WIDTH, WINDOW + tm), BF16),
            pltpu.VMEM((SWA_HEADS, 2 * WINDOW, WINDOW), F32),
            pltpu.VMEM((D_MODEL, tm), BF16),
        ],
        compiler_params=pltpu.CompilerParams(
            dimension_semantics=("arbitrary", "arbitrary"), vmem_limit_bytes=VMEM_LIMIT_BYTES),
        name=f"swa_mixer_{layer}",
    )(p["sinks_b"][j], x, p["g_mix_pre"], p["w_in_b"], kpad, kpad, vt, vt, p["mk_pad"], p["mv_t"],
      p["w_mix_out"], p["g_mix_post"], p["w_ffn_up"], p["w_ffn_down"])


def _ffn_kernel(x_ref, gpre_ref, wup_ref, wconv_ref, bconv_ref, wdown_ref, gpost_ref, *rest,
                emit_kv):
    if emit_kv:
        gkv_ref, wk2_ref, wvt_ref, o_ref, kpad_ref, vt_ref, ubuf, act_buf = rest
    else:
        o_ref, ubuf, act_buf = rest
    tm = x_ref.shape[0]
    pad = SUBLANES

    @pl.when(pl.program_id(1) == 0)
    def _():
        ubuf[:, 0:pad, :] = jnp.zeros((ubuf.shape[0], pad, LANES), F32)

    pass_rows = act_buf.shape[0]
    sub = pass_rows // FFN_SPLIT

    def part(p0, n):
        r0 = n * sub
        rows = slice(p0 + r0, p0 + r0 + sub)
        brows = slice(r0, r0 + sub)
        x = x_ref[rows, :]
        h = _rmsnorm(x, gpre_ref[...]).astype(BF16)
        yield

        def conv(col0):
            u = _dot(h, wup_ref[:, col0:col0 + FFN_CHUNK])
            return _causal_dwconv(u, ubuf, wconv_ref, bconv_ref, col0, r0, n == FFN_SPLIT - 1)

        for c in range(FFN_CHUNKS):
            gate = conv(c * FFN_CHUNK)
            val = conv(D_FF + c * FFN_CHUNK)
            act_buf[brows, c * FFN_CHUNK:(c + 1) * FFN_CHUNK] = (_gelu_tanh(gate) * val).astype(BF16)
        yield

        y = _dot(act_buf[brows, :], wdown_ref[...])
        out = x + _rmsnorm(y, gpost_ref[...])
        o_ref[rows, :] = out
        if emit_kv:
            _shared_kv(out, gkv_ref, wk2_ref, wvt_ref, kpad_ref, vt_ref, rows)

    for p0 in range(0, tm, pass_rows):
        _interleave([part(p0, n) for n in range(FFN_SPLIT)])


def _ffn_layer(x, layer, p, w_up_b, w_down_b, kv_params=None):
    bsz, seq, _ = x.shape
    emit_kv = kv_params is not None
    tm = FFN_PASS if emit_kv else TM_FFN
    out_specs = [pl.BlockSpec((None, tm, D_MODEL), lambda b, t: (b, t, 0))]
    out_shape = [jax.ShapeDtypeStruct(x.shape, F32)]
    kv_specs = []
    if emit_kv:
        kv_specs = [
            pl.BlockSpec((1, D_MODEL), lambda b, t: (0, 0)),
            pl.BlockSpec((D_MODEL, 2 * KV_WIDTH), lambda b, t: (0, 0)),
            pl.BlockSpec((KV_WIDTH, D_MODEL), lambda b, t: (0, 0)),
        ]
        out_specs += [
            pl.BlockSpec((None, tm, KPAD_WIDTH), lambda b, t: (b, t, 0)),
            pl.BlockSpec((None, KV_WIDTH, tm), lambda b, t: (b, 0, t)),
        ]
        out_shape += [
            jax.ShapeDtypeStruct((bsz, seq, KPAD_WIDTH), BF16),
            jax.ShapeDtypeStruct((bsz, KV_WIDTH, seq), BF16),
        ]
    in_specs = [
        pl.BlockSpec((None, tm, D_MODEL), lambda b, t: (b, t, 0)),
        pl.BlockSpec((None, 1, D_MODEL), lambda b, t: (layer, 0, 0)),
        pl.BlockSpec((D_MODEL, 2 * D_FF), lambda b, t: (0, 0),
                     pipeline_mode=pl.Buffered(1)),
        pl.BlockSpec((None, FFN_CONV, 2 * D_FF), lambda b, t: (layer, 0, 0)),
        pl.BlockSpec((None, 1, 2 * D_FF), lambda b, t: (layer, 0, 0)),
        pl.BlockSpec((D_FF, D_MODEL), lambda b, t: (0, 0),
                     pipeline_mode=pl.Buffered(1)),
        pl.BlockSpec((None, 1, D_MODEL), lambda b, t: (layer, 0, 0)),
    ] + kv_specs
    outs = pl.pallas_call(
        functools.partial(_ffn_kernel, emit_kv=emit_kv),
        grid=(bsz, seq // tm),
        in_specs=in_specs,
        out_specs=out_specs,
        out_shape=out_shape,
        scratch_shapes=[
            pltpu.VMEM((2 * D_FF // LANES, SUBLANES + FFN_PASS, LANES), F32),
            pltpu.VMEM((FFN_PASS, D_FF), BF16),
        ],
        compiler_params=pltpu.CompilerParams(
            dimension_semantics=("arbitrary", "arbitrary"), vmem_limit_bytes=VMEM_LIMIT_BYTES),
        name=f"ffn_{layer}",
    )(x, p["g_ffn_pre"], w_up_b, p["w_ffn_conv"], p["b_ffn_conv"], w_down_b,
      p["g_ffn_post"], *(kv_params or ()))
    return outs if emit_kv else outs[0]


def _block_diag_gates(w_r, w_i, b_r, b_i):
    na = w_r.shape[0]
    per = MXU_DIM // LRU_BLOCK

    def bd(w):
        w = w.reshape(na, GATE_CHUNKS, per, LRU_BLOCK, LRU_BLOCK)
        eye = jnp.eye(per, dtype=w.dtype)
        full = w[:, :, :, :, None, :] * eye[None, None, :, None, :, None]
        return full.reshape(na, GATE_CHUNKS, MXU_DIM, MXU_DIM)

    w = jnp.concatenate([bd(w_r), bd(w_i)], axis=-1).astype(BF16)
    b = jnp.concatenate([b_r.reshape(na, GATE_CHUNKS, 1, MXU_DIM),
                         b_i.reshape(na, GATE_CHUNKS, 1, MXU_DIM)], axis=-1)
    return w, b


def _swap_head_pairs(w_k):
    d = w_k.shape[0]
    return w_k.reshape(d, SWA_KV_HEADS // 2, 2, HEAD_DIM)[:, :, ::-1, :].reshape(d, KV_WIDTH)


@jax.jit
def kernel(x, mem, g_mix_pre, g_mix_post, g_ffn_pre, g_ffn_post, g_mem, w_mem_kv, w_mix_out,
           w_ffn_up, w_ffn_conv, b_ffn_conv, w_ffn_down, w_in_a, w_conv_a, b_conv_a,
           w_rg_r, b_rg_r, w_rg_i, b_rg_i, lru_lambda, w_in_b, sinks_b, g_kv, w_kv):
    row = lambda a: a.reshape(a.shape[0], 1, a.shape[-1])
    w_gate, b_gate = _block_diag_gates(w_rg_r, w_rg_i, b_rg_r, b_rg_i)
    p = {
        "g_mix_pre": row(g_mix_pre), "g_mix_post": row(g_mix_post),
        "g_ffn_pre": row(g_ffn_pre), "g_ffn_post": row(g_ffn_post),
        "w_mix_out": w_mix_out.astype(BF16),
        "w_ffn_up": w_ffn_up, "w_ffn_down": w_ffn_down,
        "w_ffn_conv": w_ffn_conv, "b_ffn_conv": row(b_ffn_conv),
        "w_in_a": w_in_a.astype(BF16), "w_conv_a": w_conv_a, "b_conv_a": row(b_conv_a),
        "w_gate": w_gate, "b_gate": b_gate, "lru_lambda": row(lru_lambda),
        "w_in_b": w_in_b.astype(BF16), "sinks_b": sinks_b,
    }
    p["mk_pad"], p["mv_t"] = _memkv(
        mem, row(g_mem), w_mem_kv[:, :, :MEM_WIDTH].astype(BF16),
        jnp.swapaxes(w_mem_kv[:, :, MEM_WIDTH:], 1, 2).astype(BF16))
    w_k = w_kv[:, :KV_WIDTH]
    kv_params = (g_kv.reshape(1, D_MODEL),
                 jnp.concatenate([w_k, _swap_head_pairs(w_k)], axis=-1).astype(BF16),
                 w_kv[:, KV_WIDTH:].T.astype(BF16))
    kpad = vt = None
    for layer in range(DEPTH):
        if layer < N_A_LAYERS:
            x, w_up_b, w_down_b = _rglru_layer(x, layer, layer, p)
        else:
            x, w_up_b, w_down_b = _swa_layer(x, kpad, vt, layer, layer - N_A_LAYERS, p)
        if layer == N_A_LAYERS - 1:
            x, kpad, vt = _ffn_layer(x, layer, p, w_up_b, w_down_b, kv_params)
        else:
            x = _ffn_layer(x, layer, p, w_up_b, w_down_b)
    return x
```

```python
import functools
import math

import jax
import jax.numpy as jnp
import numpy as np
from jax import lax
from jax.experimental import pallas as pl
from jax.experimental.pallas import tpu as pltpu

D_MODEL = 1024
DEPTH = 4
N_A_LAYERS = DEPTH // 2
HEAD_DIM = 64
MEM_LEN = 256
MEM_HEADS = 4
MEM_WIDTH = MEM_HEADS * HEAD_DIM
MIX_WIDTH = D_MODEL - MEM_WIDTH
LRU_WIDTH = MIX_WIDTH
LRU_BLOCK = 64
LRU_CONV = 4
LRU_C = 8.0
SWA_HEADS = MIX_WIDTH // HEAD_DIM
SWA_KV_HEADS = 4
SWA_GROUP = SWA_HEADS // SWA_KV_HEADS
KV_WIDTH = SWA_KV_HEADS * HEAD_DIM
WINDOW = 128
D_FF = 2816
FFN_CONV = 3
EPS = 1e-6

SUBLANES = 8
LANES = 128
MXU_DIM = 256
VMEM_LIMIT_BYTES = 56 * 1024 * 1024

HEADS_PER_TILE = LANES // HEAD_DIM
KPAD_WIDTH = 2 * SWA_KV_HEADS * LANES
MKPAD_WIDTH = MEM_HEADS * LANES

GATE_CHUNKS = LRU_WIDTH // MXU_DIM
FFN_CHUNK = 256
FFN_CHUNKS = D_FF // FFN_CHUNK

TM_MIX = 1024
MIX_SPLIT = 4
FFN_SPLIT = 1
SWA_SPLIT = 1
CAST_ROWS_UP = 32
CAST_ROWS_DOWN = 128
SEG_GAP = 4
TM_FFN = 1024
FFN_RING = 8

BF16 = jnp.bfloat16
F32 = jnp.float32
QK_SCALE = HEAD_DIM ** -0.5


def _alibi_slopes(n):
    def pow2_slopes(m):
        start = 2.0 ** (-8.0 / m)
        return [start ** (i + 1) for i in range(m)]
    c = 2 ** int(math.floor(math.log2(n)))
    s = pow2_slopes(c)
    if c != n:
        s = s + pow2_slopes(2 * c)[0::2][: n - c]
    return [float(np.float32(v)) for v in s]


SLOPES = _alibi_slopes(SWA_HEADS)


def _rmsnorm(x, g):
    ms = jnp.mean(x * x, axis=-1, keepdims=True)
    return x * lax.rsqrt(ms + EPS) * g


def _gelu_tanh(x):
    c = math.sqrt(2.0 / math.pi)
    return 0.5 * x * (1.0 + jnp.tanh(c * (x + 0.044715 * (x * x * x))))


def _dot(a, b):
    return jnp.dot(a, b, preferred_element_type=F32)


def _dot_nt(a, b):
    return lax.dot_general(a, b, (((1,), (1,)), ((), ())), preferred_element_type=F32)


def _dot_tn(a, b):
    return lax.dot_general(a, b, (((0,), (0,)), ((), ())), preferred_element_type=F32)


def _half_tiles(k, parities):
    upper = lax.broadcasted_iota(jnp.int32, (k.shape[0], LANES), 1) >= HEAD_DIM
    tiles = []
    for i, par in enumerate(parities):
        t = k[:, i * LANES:(i + 1) * LANES]
        tiles.append(jnp.where(upper if par else ~upper, t, 0.0))
    return jnp.concatenate(tiles, axis=-1).astype(BF16)


def _softmax_t(s_t, extra=None):
    m = jnp.max(s_t, axis=0, keepdims=True)
    if extra is not None:
        m = jnp.maximum(m, extra)
    p = jnp.exp(s_t - m)
    d = jnp.sum(p, axis=0, keepdims=True)
    if extra is not None:
        d = d + jnp.exp(extra - m)
    return (p * (1.0 / d)).astype(BF16)


def _causal_dwconv(u, ubuf, w_ref, b_ref, col0, row0=0, last=True):
    m = u.shape[0]
    taps = w_ref.shape[0]
    base = SUBLANES + row0
    outs = []
    for i in range(u.shape[1] // LANES):
        cols = slice(col0 + i * LANES, col0 + (i + 1) * LANES)
        slab = col0 // LANES + i
        ui = u[:, i * LANES:(i + 1) * LANES]
        ubuf[slab, base:base + m, :] = ui
        out = ui * w_ref[taps - 1:taps, cols] + b_ref[:, cols]
        for k in range(taps - 1):
            shift = taps - 1 - k
            out = out + ubuf[slab, base - shift:base - shift + m, :] * w_ref[k:k + 1, cols]
        if last:
            ubuf[slab, 0:SUBLANES, :] = ui[m - SUBLANES:, :]
        outs.append(out)
    return jnp.concatenate(outs, axis=-1)


def _memory_scores_t(q16, tile0, mk_ref):
    s_ts = []
    for hh in range(MEM_HEADS):
        tile = tile0 + hh // HEADS_PER_TILE
        s_ts.append(_dot_nt(mk_ref[:, hh * LANES:(hh + 1) * LANES],
                            q16[:, tile * LANES:(tile + 1) * LANES]))
    return s_ts


def _memory_values_t(p_ts, mvt_ref, yt_ref, row0, cols):
    for hh in range(MEM_HEADS):
        o_t = _dot(mvt_ref[hh * HEAD_DIM:(hh + 1) * HEAD_DIM, :], p_ts[hh])
        yt_ref[row0 + hh * HEAD_DIM:row0 + (hh + 1) * HEAD_DIM, cols] = o_t.astype(BF16)


def _interleave(parts):
    live = list(parts)
    while live:
        nxt = []
        for part in live:
            try:
                next(part)
                nxt.append(part)
            except StopIteration:
                pass
        live = nxt


def _ffn_weight_cast_specs(layer, bsz, nt):
    steps = bsz * nt
    assert steps * CAST_ROWS_UP == D_MODEL and steps * CAST_ROWS_DOWN >= D_FF and D_FF % CAST_ROWS_DOWN == 0
    up_blk = lambda b, t: b * nt + t
    down_blk = lambda b, t: jnp.minimum(b * nt + t, D_FF // CAST_ROWS_DOWN - 1)
    in_specs = [pl.BlockSpec((None, CAST_ROWS_UP, 2 * D_FF), lambda b, t: (layer, up_blk(b, t), 0)),
                pl.BlockSpec((None, CAST_ROWS_DOWN, D_MODEL), lambda b, t: (layer, down_blk(b, t), 0))]
    out_specs = [pl.BlockSpec((CAST_ROWS_UP, 2 * D_FF), lambda b, t: (up_blk(b, t), 0)),
                 pl.BlockSpec((CAST_ROWS_DOWN, D_MODEL), lambda b, t: (down_blk(b, t), 0))]
    out_shape = [jax.ShapeDtypeStruct((D_MODEL, 2 * D_FF), BF16),
                 jax.ShapeDtypeStruct((D_FF, D_MODEL), BF16)]
    return in_specs, out_specs, out_shape


def _memkv_kernel(mem_ref, g_ref, wk_ref, wvt_ref, mk_ref, mvt_ref):
    m = mem_ref[...]
    ms = jnp.mean(m * m, axis=-1, keepdims=True)
    mn = m * lax.rsqrt(ms + EPS)
    for l in range(DEPTH):
        mnl = (mn * g_ref[l]).astype(BF16)
        k = _dot(mnl, wk_ref[l])
        k4 = jnp.concatenate([k[:, (hh // 2) * LANES:(hh // 2 + 1) * LANES]
                              for hh in range(MEM_HEADS)], axis=-1)
        mk_ref[l] = _half_tiles(k4, [hh % 2 for hh in range(MEM_HEADS)])
        mvt_ref[l] = _dot_nt(wvt_ref[l], mnl).astype(BF16)


def _memkv(mem, g_mem, wk_b, wvt_b):
    bsz = mem.shape[0]
    return pl.pallas_call(
        _memkv_kernel,
        grid=(bsz,),
        in_specs=[
            pl.BlockSpec((None, MEM_LEN, D_MODEL), lambda b: (b, 0, 0)),
            pl.BlockSpec((DEPTH, 1, D_MODEL), lambda b: (0, 0, 0)),
            pl.BlockSpec((DEPTH, D_MODEL, MEM_WIDTH), lambda b: (0, 0, 0)),
            pl.BlockSpec((DEPTH, MEM_WIDTH, D_MODEL), lambda b: (0, 0, 0)),
        ],
        out_specs=[
            pl.BlockSpec((DEPTH, None, MEM_LEN, MKPAD_WIDTH), lambda b: (0, b, 0, 0)),
            pl.BlockSpec((DEPTH, None, MEM_WIDTH, MEM_LEN), lambda b: (0, b, 0, 0)),
        ],
        out_shape=[
            jax.ShapeDtypeStruct((DEPTH, bsz, MEM_LEN, MKPAD_WIDTH), BF16),
            jax.ShapeDtypeStruct((DEPTH, bsz, MEM_WIDTH, MEM_LEN), BF16),
        ],
        compiler_params=pltpu.CompilerParams(dimension_semantics=("arbitrary",)),
        name="memkv",
    )(mem, g_mem, wk_b, wvt_b)


def _shared_kv(x, g_ref, wk2_ref, wvt_ref, kpad_ref, vt_ref, rows):
    h = _rmsnorm(x, g_ref[...]).astype(BF16)
    k2 = _dot(h, wk2_ref[...])
    nat = [k2[:, i * LANES:(i + 1) * LANES] for i in range(2)]
    swp = [k2[:, KV_WIDTH + i * LANES:KV_WIDTH + (i + 1) * LANES] for i in range(2)]
    lo = jnp.concatenate([nat[0], swp[0], nat[1], swp[1]], axis=-1)
    hi = jnp.concatenate([swp[0], nat[0], swp[1], nat[1]], axis=-1)
    kpad_ref[rows, :SWA_KV_HEADS * LANES] = _half_tiles(lo, [0] * SWA_KV_HEADS)
    kpad_ref[rows, SWA_KV_HEADS * LANES:] = _half_tiles(hi, [1] * SWA_KV_HEADS)
    vt_ref[:, rows] = _dot_nt(wvt_ref[...], h).astype(BF16)


def _rglru_kernel(x_ref, gpre_ref, win_ref, wconv_ref, bconv_ref, wgate_ref, bgate_ref,
                  lam_ref, mk_ref, mvt_ref, wout_ref, gpost_ref, wupf_ref, wdnf_ref,
                  o_ref, wupb_ref, wdnb_ref, ubuf, a_seg, b_seg, h_seg, hcarry, ymt_buf):
    wupb_ref[...] = wupf_ref[...].astype(BF16)
    wdnb_ref[...] = wdnf_ref[...].astype(BF16)
    tm = x_ref.shape[0]
    pad = SUBLANES
    n_slab = LRU_WIDTH // LANES
    seg_len = tm // MIX_SPLIT // SUBLANES
    pitch = seg_len + SEG_GAP

    @pl.when(pl.program_id(1) == 0)
    def _():
        ubuf[:, 0:pad, :] = jnp.zeros((ubuf.shape[0], pad, LANES), F32)
        hcarry[...] = jnp.zeros_like(hcarry)

    lam = lam_ref[...]
    softplus_neg_lam = jnp.maximum(-lam, 0.0) + jnp.log1p(jnp.exp(-jnp.abs(lam)))
    half_decay = (0.5 * LRU_C) * softplus_neg_lam
    row = lax.broadcasted_iota(jnp.int32, (SUBLANES, LRU_WIDTH), 0)
    state = {"carry": hcarry[...]}
    sub = tm // MIX_SPLIT

    def part(n):
        r0 = n * sub
        rows = slice(r0, r0 + sub)
        x = x_ref[rows, :]
        h = _rmsnorm(x, gpre_ref[...]).astype(BF16)
        proj = _dot(h, win_ref[...])
        u_gate = proj[:, :LRU_WIDTH]
        u_x = proj[:, LRU_WIDTH:2 * LRU_WIDTH]
        q16 = (proj[:, 2 * LRU_WIDTH:] * QK_SCALE).astype(BF16)
        s_ts = _memory_scores_t(q16, 0, mk_ref)
        yield

        xc = _causal_dwconv(u_x, ubuf, wconv_ref, bconv_ref, 0, r0, n == MIX_SPLIT - 1)
        xcb = xc.astype(BF16)
        gates = [_dot(xcb[:, c * MXU_DIM:(c + 1) * MXU_DIM], wgate_ref[c]) + bgate_ref[c]
                 for c in range(GATE_CHUNKS)]
        _memory_values_t([_softmax_t(s_t) for s_t in s_ts], mvt_ref, ymt_buf, 0, rows)
        yield

        t_r = jnp.concatenate([jnp.tanh(0.5 * g[:, :MXU_DIM]) for g in gates], axis=-1)
        t_i = jnp.concatenate([jnp.tanh(0.5 * g[:, MXU_DIM:]) for g in gates], axis=-1)
        nla = half_decay * t_r + half_decay
        a = jnp.exp(-nla)
        z = jnp.tanh(nla) * (a * a + 1.0)
        gain = jnp.where(z > 0.0, z * lax.rsqrt(z), 0.0)
        b = gain * ((0.5 * t_i + 0.5) * xc)

        for sl in range(n_slab):
            lanes = slice(sl * LANES, (sl + 1) * LANES)
            for s in range(SUBLANES):
                seg = slice(s * pitch, s * pitch + seg_len)
                src = slice(s * seg_len, (s + 1) * seg_len)
                a_seg[n * n_slab + sl, seg, :] = a[src, lanes]
                b_seg[n * n_slab + sl, seg, :] = b[src, lanes]

        def seg_row(ref, j):
            return jnp.concatenate(
                [ref[n * n_slab + sl, pl.ds(j, SUBLANES, stride=pitch), :] for sl in range(n_slab)], axis=-1)

        a_end = seg_row(a_seg, 0)
        h_end = seg_row(b_seg, 0)
        for j in range(1, seg_len):
            aj = seg_row(a_seg, j)
            h_end = aj * h_end + seg_row(b_seg, j)
            a_end = aj * a_end
        for k in (1, 2, 4):
            keep = row >= k
            a_sh = jnp.where(keep, pltpu.roll(a_end, k, 0), 1.0)
            h_sh = jnp.where(keep, pltpu.roll(h_end, k, 0), 0.0)
            h_end = a_end * h_sh + h_end
            a_end = a_end * a_sh
        ends = h_end + a_end * state["carry"]
        hj = jnp.where(row >= 1, pltpu.roll(ends, 1, 0), state["carry"])
        state["carry"] = jnp.broadcast_to(ends[SUBLANES - 1:SUBLANES, :], (SUBLANES, LRU_WIDTH))
        for j in range(seg_len):
            hj = seg_row(a_seg, j) * hj + seg_row(b_seg, j)
            for sl in range(n_slab):
                h_seg[n * n_slab + sl, pl.ds(j, SUBLANES, stride=pitch), :] = hj[:, sl * LANES:(sl + 1) * LANES]
        yield

        h_nat = jnp.concatenate(
            [jnp.concatenate([h_seg[n * n_slab + sl, s * pitch:s * pitch + seg_len, :]
                              for s in range(SUBLANES)], axis=0) for sl in range(n_slab)], axis=-1)
        y_main = (h_nat * _gelu_tanh(u_gate)).astype(BF16)
        y = (_dot(y_main, wout_ref[:MIX_WIDTH, :])
             + _dot_tn(ymt_buf[:, rows], wout_ref[MIX_WIDTH:, :]))
        o_ref[rows, :] = x + _rmsnorm(y, gpost_ref[...])

    _interleave([part(n) for n in range(MIX_SPLIT)])
    hcarry[...] = state["carry"]


def _rglru_layer(x, layer, j, p):
    bsz, seq, _ = x.shape
    tm = TM_MIX
    n_in = 2 * LRU_WIDTH + MEM_WIDTH
    in_specs = [
        pl.BlockSpec((None, tm, D_MODEL), lambda b, t: (b, t, 0)),
        pl.BlockSpec((None, 1, D_MODEL), lambda b, t: (layer, 0, 0)),
        pl.BlockSpec((None, D_MODEL, n_in), lambda b, t: (j, 0, 0)),
        pl.BlockSpec((None, LRU_CONV, LRU_WIDTH), lambda b, t: (j, 0, 0)),
        pl.BlockSpec((None, 1, LRU_WIDTH), lambda b, t: (j, 0, 0)),
        pl.BlockSpec((None, GATE_CHUNKS, MXU_DIM, 2 * MXU_DIM), lambda b, t: (j, 0, 0, 0)),
        pl.BlockSpec((None, GATE_CHUNKS, 1, 2 * MXU_DIM), lambda b, t: (j, 0, 0, 0)),
        pl.BlockSpec((None, 1, LRU_WIDTH), lambda b, t: (j, 0, 0)),
        pl.BlockSpec((None, None, MEM_LEN, MKPAD_WIDTH), lambda b, t: (layer, b, 0, 0)),
        pl.BlockSpec((None, None, MEM_WIDTH, MEM_LEN), lambda b, t: (layer, b, 0, 0)),
        pl.BlockSpec((None, D_MODEL, D_MODEL), lambda b, t: (layer, 0, 0)),
        pl.BlockSpec((None, 1, D_MODEL), lambda b, t: (layer, 0, 0)),
    ]
    cast_in, cast_out, cast_shape = _ffn_weight_cast_specs(layer, bsz, seq // tm)
    return pl.pallas_call(
        _rglru_kernel,
        grid=(bsz, seq // tm),
        in_specs=in_specs + cast_in,
        out_specs=[pl.BlockSpec((None, tm, D_MODEL), lambda b, t: (b, t, 0))] + cast_out,
        out_shape=[jax.ShapeDtypeStruct(x.shape, F32)] + cast_shape,
        scratch_shapes=[
            pltpu.VMEM((LRU_WIDTH // LANES, SUBLANES + tm, LANES), F32),
            pltpu.VMEM((MIX_SPLIT * LRU_WIDTH // LANES, tm // MIX_SPLIT + SUBLANES * SEG_GAP, LANES), F32),
            pltpu.VMEM((MIX_SPLIT * LRU_WIDTH // LANES, tm // MIX_SPLIT + SUBLANES * SEG_GAP, LANES), F32),
            pltpu.VMEM((MIX_SPLIT * LRU_WIDTH // LANES, tm // MIX_SPLIT + SUBLANES * SEG_GAP, LANES), F32),
            pltpu.VMEM((SUBLANES, LRU_WIDTH), F32),
            pltpu.VMEM((MEM_WIDTH, tm), BF16),
        ],
        compiler_params=pltpu.CompilerParams(
            dimension_semantics=("arbitrary", "arbitrary"), vmem_limit_bytes=VMEM_LIMIT_BYTES),
        name=f"rglru_mixer_{layer}",
    )(x, p["g_mix_pre"], p["w_in_a"], p["w_conv_a"], p["b_conv_a"], p["w_gate"], p["b_gate"],
      p["lru_lambda"], p["mk_pad"], p["mv_t"], p["w_mix_out"], p["g_mix_post"],
      p["w_ffn_up"], p["w_ffn_down"])


def _swa_kernel(sinks_ref, x_ref, gpre_ref, win_ref, kp_ref, kc_ref, vtp_ref, vtc_ref,
                mk_ref, mvt_ref, wout_ref, gpost_ref, wupf_ref, wdnf_ref,
                o_ref, wupb_ref, wdnb_ref, k_buf, vt_buf, bias_buf, yt_buf):
    wupb_ref[...] = wupf_ref[...].astype(BF16)
    wdnb_ref[...] = wdnf_ref[...].astype(BF16)
    tm = x_ref.shape[0]
    band_keys = 2 * WINDOW

    @pl.when((pl.program_id(0) == 0) & (pl.program_id(1) == 0))
    def _():
        kj = lax.broadcasted_iota(jnp.int32, (band_keys, WINDOW), 0)
        qi = lax.broadcasted_iota(jnp.int32, (band_keys, WINDOW), 1)
        dist = qi + WINDOW - kj
        in_window = (dist >= 0) & (dist < WINDOW)
        dist_f = dist.astype(F32)
        for hd in range(SWA_HEADS):
            bias_buf[hd] = jnp.where(in_window, -SLOPES[hd] * dist_f, -jnp.inf)

    k_buf[0:WINDOW, :] = kp_ref[...]
    k_buf[WINDOW:, :] = kc_ref[...]
    vt_buf[:, 0:WINDOW] = vtp_ref[...]
    vt_buf[:, WINDOW:] = vtc_ref[...]

    kj = lax.broadcasted_iota(jnp.int32, (band_keys, WINDOW), 0)
    no_prev = jnp.where(kj < jnp.where(pl.program_id(1) == 0, WINDOW, 0), -jnp.inf, 0.0)
    sub = tm // SWA_SPLIT

    def part(n):
        r0 = n * sub
        rows = slice(r0, r0 + sub)
        x = x_ref[rows, :]
        h = _rmsnorm(x, gpre_ref[...]).astype(BF16)
        q16 = (_dot(h, win_ref[...]) * QK_SCALE).astype(BF16)
        yield

        units = [(hd, qb) for hd in range(SWA_HEADS) for qb in range(r0 // WINDOW, (r0 + sub) // WINDOW)]
        s_ts = []
        for hd, qb in units:
            if (qb - r0 // WINDOW) % 2:
                continue
            g = hd // SWA_GROUP
            ktile = (hd % HEADS_PER_TILE) * SWA_KV_HEADS + g
            qtile = hd // HEADS_PER_TILE
            pair = _dot_nt(
                k_buf[qb * WINDOW:(qb + 1) * WINDOW + band_keys, ktile * LANES:(ktile + 1) * LANES],
                q16[qb * WINDOW - r0:(qb + 2) * WINDOW - r0, qtile * LANES:(qtile + 1) * LANES])
            s_ts.append(pair[:band_keys, :WINDOW])
            s_ts.append(pair[WINDOW:, WINDOW:])
        ms_ts = _memory_scores_t(q16, MIX_WIDTH // LANES, mk_ref)
        yield

        p_ts = []
        for (hd, qb), s_t in zip(units, s_ts):
            s_t = s_t + bias_buf[hd]
            if qb == 0:
                s_t = s_t + no_prev
            p_ts.append(_softmax_t(s_t, jnp.full((1, WINDOW), sinks_ref[hd], F32)))
        mp_ts = [_softmax_t(s_t) for s_t in ms_ts]
        yield

        for (hd, qb), p_t in zip(units, p_ts):
            g = hd // SWA_GROUP
            o_t = _dot(vt_buf[g * HEAD_DIM:(g + 1) * HEAD_DIM, qb * WINDOW:qb * WINDOW + band_keys], p_t)
            yt_buf[hd * HEAD_DIM:(hd + 1) * HEAD_DIM, qb * WINDOW:(qb + 1) * WINDOW] = o_t.astype(BF16)
        _memory_values_t(mp_ts, mvt_ref, yt_buf, MIX_WIDTH, rows)
        yield

        y = _dot_tn(yt_buf[:, rows], wout_ref[...])
        o_ref[rows, :] = x + _rmsnorm(y, gpost_ref[...])

    _interleave([part(n) for n in range(SWA_SPLIT)])


def _swa_layer(x, kpad, vt, layer, j, p):
    bsz, seq, _ = x.shape
    tm = TM_MIX
    per = tm // WINDOW
    prev_blk = lambda t: jnp.maximum(t * per - 1, 0)
    in_specs = [
        pl.BlockSpec(memory_space=pltpu.SMEM),
        pl.BlockSpec((None, tm, D_MODEL), lambda b, t: (b, t, 0)),
        pl.BlockSpec((None, 1, D_MODEL), lambda b, t: (layer, 0, 0)),
        pl.BlockSpec((None, D_MODEL, D_MODEL), lambda b, t: (j, 0, 0)),
        pl.BlockSpec((None, WINDOW, KPAD_WIDTH), lambda b, t: (b, prev_blk(t), 0)),
        pl.BlockSpec((None, tm, KPAD_WIDTH), lambda b, t: (b, t, 0)),
        pl.BlockSpec((None, KV_WIDTH, WINDOW), lambda b, t: (b, 0, prev_blk(t))),
        pl.BlockSpec((None, KV_WIDTH, tm), lambda b, t: (b, 0, t)),
        pl.BlockSpec((None, None, MEM_LEN, MKPAD_WIDTH), lambda b, t: (layer, b, 0, 0)),
        pl.BlockSpec((None, None, MEM_WIDTH, MEM_LEN), lambda b, t: (layer, b, 0, 0)),
        pl.BlockSpec((None, D_MODEL, D_MODEL), lambda b, t: (layer, 0, 0)),
        pl.BlockSpec((None, 1, D_MODEL), lambda b, t: (layer, 0, 0)),
    ]
    cast_in, cast_out, cast_shape = _ffn_weight_cast_specs(layer, bsz, seq // tm)
    return pl.pallas_call(
        _swa_kernel,
        grid=(bsz, seq // tm),
        in_specs=in_specs + cast_in,
        out_specs=[pl.BlockSpec((None, tm, D_MODEL), lambda b, t: (b, t, 0))] + cast_out,
        out_shape=[jax.ShapeDtypeStruct(x.shape, F32)] + cast_shape,
        scratch_shapes=[
            pltpu.VMEM((WINDOW + tm, KPAD_WIDTH), BF16),
            pltpu.VMEM((KV_WIDTH, WINDOW + tm), BF16),
            pltpu.VMEM((SWA_HEADS, 2 * WINDOW, WINDOW), F32),
            pltpu.VMEM((D_MODEL, tm), BF16),
        ],
        compiler_params=pltpu.CompilerParams(
            dimension_semantics=("arbitrary", "arbitrary"), vmem_limit_bytes=VMEM_LIMIT_BYTES),
        name=f"swa_mixer_{layer}",
    )(p["sinks_b"][j], x, p["g_mix_pre"], p["w_in_b"], kpad, kpad, vt, vt, p["mk_pad"], p["mv_t"],
      p["w_mix_out"], p["g_mix_post"], p["w_ffn_up"], p["w_ffn_down"])


def _ffn_kernel(x_ref, gpre_ref, wup_ref, wconv_ref, bconv_ref, wdown_ref, gpost_ref, *rest,
                emit_kv):
    if emit_kv:
        gkv_ref, wk2_ref, wvt_ref, o_ref, kpad_ref, vt_ref, ubuf, tails, act_buf = rest
    else:
        o_ref, ubuf, tails, act_buf = rest
    tm = x_ref.shape[0]
    pad = SUBLANES

    @pl.when(pl.program_id(1) == 0)
    def _():
        tails[...] = jnp.zeros_like(tails)

    sub = tm // FFN_SPLIT
    ring = ubuf.shape[0]
    slot = [0]

    def part(n):
        r0 = n * sub
        rows = slice(r0, r0 + sub)
        x = x_ref[rows, :]
        h = _rmsnorm(x, gpre_ref[...]).astype(BF16)
        yield

        def conv(col0):
            u = _dot(h, wup_ref[:, col0:col0 + FFN_CHUNK])
            taps = wconv_ref.shape[0]
            outs = []
            for i in range(FFN_CHUNK // LANES):
                cols = slice(col0 + i * LANES, col0 + (i + 1) * LANES)
                col_slab = col0 // LANES + i
                s = slot[0] % ring
                slot[0] += 1
                ui = u[:, i * LANES:(i + 1) * LANES]
                ubuf[s, 0:pad, :] = tails[col_slab]
                ubuf[s, pad:pad + sub, :] = ui
                out = ui * wconv_ref[taps - 1:taps, cols] + bconv_ref[:, cols]
                for k in range(taps - 1):
                    shift = taps - 1 - k
                    out = out + ubuf[s, pad - shift:pad - shift + sub, :] * wconv_ref[k:k + 1, cols]
                tails[col_slab] = ui[sub - pad:, :]
                outs.append(out)
            return jnp.concatenate(outs, axis=-1)

        for c in range(FFN_CHUNKS):
            gate = conv(c * FFN_CHUNK)
            val = conv(D_FF + c * FFN_CHUNK)
            act_buf[rows, c * FFN_CHUNK:(c + 1) * FFN_CHUNK] = (_gelu_tanh(gate) * val).astype(BF16)
        yield

        y = _dot(act_buf[rows, :], wdown_ref[...])
        out = x + _rmsnorm(y, gpost_ref[...])
        o_ref[rows, :] = out
        if emit_kv:
            _shared_kv(out, gkv_ref, wk2_ref, wvt_ref, kpad_ref, vt_ref, rows)

    _interleave([part(n) for n in range(FFN_SPLIT)])


def _ffn_layer(x, layer, p, w_up_b, w_down_b, kv_params=None):
    bsz, seq, _ = x.shape
    tm = TM_FFN
    emit_kv = kv_params is not None
    out_specs = [pl.BlockSpec((None, tm, D_MODEL), lambda b, t: (b, t, 0))]
    out_shape = [jax.ShapeDtypeStruct(x.shape, F32)]
    kv_specs = []
    if emit_kv:
        kv_specs = [
            pl.BlockSpec((1, D_MODEL), lambda b, t: (0, 0)),
            pl.BlockSpec((D_MODEL, 2 * KV_WIDTH), lambda b, t: (0, 0)),
            pl.BlockSpec((KV_WIDTH, D_MODEL), lambda b, t: (0, 0)),
        ]
        out_specs += [
            pl.BlockSpec((None, tm, KPAD_WIDTH), lambda b, t: (b, t, 0)),
            pl.BlockSpec((None, KV_WIDTH, tm), lambda b, t: (b, 0, t)),
        ]
        out_shape += [
            jax.ShapeDtypeStruct((bsz, seq, KPAD_WIDTH), BF16),
            jax.ShapeDtypeStruct((bsz, KV_WIDTH, seq), BF16),
        ]
    in_specs = [
        pl.BlockSpec((None, tm, D_MODEL), lambda b, t: (b, t, 0)),
        pl.BlockSpec((None, 1, D_MODEL), lambda b, t: (layer, 0, 0)),
        pl.BlockSpec((D_MODEL, 2 * D_FF), lambda b, t: (0, 0),
                     pipeline_mode=pl.Buffered(1)),
        pl.BlockSpec((None, FFN_CONV, 2 * D_FF), lambda b, t: (layer, 0, 0)),
        pl.BlockSpec((None, 1, 2 * D_FF), lambda b, t: (layer, 0, 0)),
        pl.BlockSpec((D_FF, D_MODEL), lambda b, t: (0, 0),
                     pipeline_mode=pl.Buffered(1)),
        pl.BlockSpec((None, 1, D_MODEL), lambda b, t: (layer, 0, 0)),
    ] + kv_specs
    outs = pl.pallas_call(
        functools.partial(_ffn_kernel, emit_kv=emit_kv),
        grid=(bsz, seq // tm),
        in_specs=in_specs,
        out_specs=out_specs,
        out_shape=out_shape,
        scratch_shapes=[
            pltpu.VMEM((FFN_RING, SUBLANES + tm, LANES), F32),
            pltpu.VMEM((2 * D_FF // LANES, SUBLANES, LANES), F32),
            pltpu.VMEM((tm, D_FF), BF16),
        ],
        compiler_params=pltpu.CompilerParams(
            dimension_semantics=("arbitrary", "arbitrary"), vmem_limit_bytes=VMEM_LIMIT_BYTES),
        name=f"ffn_{layer}",
    )(x, p["g_ffn_pre"], w_up_b, p["w_ffn_conv"], p["b_ffn_conv"], w_down_b,
      p["g_ffn_post"], *(kv_params or ()))
    return outs if emit_kv else outs[0]


def _block_diag_gates(w_r, w_i, b_r, b_i):
    na = w_r.shape[0]
    per = MXU_DIM // LRU_BLOCK

    def bd(w):
        w = w.reshape(na, GATE_CHUNKS, per, LRU_BLOCK, LRU_BLOCK)
        eye = jnp.eye(per, dtype=w.dtype)
        full = w[:, :, :, :, None, :] * eye[None, None, :, None, :, None]
        return full.reshape(na, GATE_CHUNKS, MXU_DIM, MXU_DIM)

    w = jnp.concatenate([bd(w_r), bd(w_i)], axis=-1).astype(BF16)
    b = jnp.concatenate([b_r.reshape(na, GATE_CHUNKS, 1, MXU_DIM),
                         b_i.reshape(na, GATE_CHUNKS, 1, MXU_DIM)], axis=-1)
    return w, b


def _swap_head_pairs(w_k):
    d = w_k.shape[0]
    return w_k.reshape(d, SWA_KV_HEADS // 2, 2, HEAD_DIM)[:, :, ::-1, :].reshape(d, KV_WIDTH)


@jax.jit
def kernel(x, mem, g_mix_pre, g_mix_post, g_ffn_pre, g_ffn_post, g_mem, w_mem_kv, w_mix_out,
           w_ffn_up, w_ffn_conv, b_ffn_conv, w_ffn_down, w_in_a, w_conv_a, b_conv_a,
           w_rg_r, b_rg_r, w_rg_i, b_rg_i, lru_lambda, w_in_b, sinks_b, g_kv, w_kv):
    row = lambda a: a.reshape(a.shape[0], 1, a.shape[-1])
    w_gate, b_gate = _block_diag_gates(w_rg_r, w_rg_i, b_rg_r, b_rg_i)
    p = {
        "g_mix_pre": row(g_mix_pre), "g_mix_post": row(g_mix_post),
        "g_ffn_pre": row(g_ffn_pre), "g_ffn_post": row(g_ffn_post),
        "w_mix_out": w_mix_out.astype(BF16),
        "w_ffn_up": w_ffn_up, "w_ffn_down": w_ffn_down,
        "w_ffn_conv": w_ffn_conv, "b_ffn_conv": row(b_ffn_conv),
        "w_in_a": w_in_a.astype(BF16), "w_conv_a": w_conv_a, "b_conv_a": row(b_conv_a),
        "w_gate": w_gate, "b_gate": b_gate, "lru_lambda": row(lru_lambda),
        "w_in_b": w_in_b.astype(BF16), "sinks_b": sinks_b,
    }
    p["mk_pad"], p["mv_t"] = _memkv(
        mem, row(g_mem), w_mem_kv[:, :, :MEM_WIDTH].astype(BF16),
        jnp.swapaxes(w_mem_kv[:, :, MEM_WIDTH:], 1, 2).astype(BF16))
    w_k = w_kv[:, :KV_WIDTH]
    kv_params = (g_kv.reshape(1, D_MODEL),
                 jnp.concatenate([w_k, _swap_head_pairs(w_k)], axis=-1).astype(BF16),
                 w_kv[:, KV_WIDTH:].T.astype(BF16))
    kpad = vt = None
    for layer in range(DEPTH):
        if layer < N_A_LAYERS:
            x, w_up_b, w_down_b = _rglru_layer(x, layer, layer, p)
        else:
            x, w_up_b, w_down_b = _swa_layer(x, kpad, vt, layer, layer - N_A_LAYERS, p)
        if layer == N_A_LAYERS - 1:
            x, kpad, vt = _ffn_layer(x, layer, p, w_up_b, w_down_b, kv_params)
        else:
            x = _ffn_layer(x, layer, p, w_up_b, w_down_b)
    return x
```
